```python
import jax, jax.numpy as jnp
from jax import lax
import numpy as np

D_MODEL = 1024
BATCH = 4
SEQ = 4096
DEPTH = 1

N_META = 16
GRID_W = 64
HEAD_DIM = 64
N_DIR = 2
RWKV_HEADS = 8
RWKV_WIDTH = RWKV_HEADS * HEAD_DIM
DECAY_LORA = 64
ICL_LORA = 64
GATE_LORA = 160
LNX_EPS = 64e-5
ATTN_Q_HEADS = 8
ATTN_KV_HEADS = 2
ATTN_GROUP = ATTN_Q_HEADS // ATTN_KV_HEADS
ATTN_Q_WIDTH = ATTN_Q_HEADS * HEAD_DIM
ATTN_KV_WIDTH = ATTN_KV_HEADS * HEAD_DIM
Q_BLOCK = 128
ROPE_THETA = 10000.0
AXIS_DIM = HEAD_DIM // 2
AXIS_FREQS = AXIS_DIM // 2
D_FF = 4 * D_MODEL
NORM_EPS = 1e-6
RWKV_IN_WIDTH = 3 * RWKV_WIDTH + N_DIR * DECAY_LORA + N_DIR * ICL_LORA + GATE_LORA
IN_WIDTH = RWKV_IN_WIDTH + ATTN_Q_WIDTH + 2 * ATTN_KV_WIDTH + 2 * D_MODEL
IN_SPLITS = [RWKV_IN_WIDTH,
             RWKV_IN_WIDTH + ATTN_Q_WIDTH,
             RWKV_IN_WIDTH + ATTN_Q_WIDTH + ATTN_KV_WIDTH,
             RWKV_IN_WIDTH + ATTN_Q_WIDTH + 2 * ATTN_KV_WIDTH,
             RWKV_IN_WIDTH + ATTN_Q_WIDTH + 2 * ATTN_KV_WIDTH + D_MODEL]
RWKV_SPLITS = [RWKV_WIDTH, 2 * RWKV_WIDTH, 3 * RWKV_WIDTH,
               3 * RWKV_WIDTH + N_DIR * DECAY_LORA,
               3 * RWKV_WIDTH + N_DIR * DECAY_LORA + N_DIR * ICL_LORA]

kernel_name = "hybrid_rwkv7_axial_gqa_gated_encoder"


def rms_norm(x, g, eps=NORM_EPS):
    xf = x.astype(jnp.float32)
    y = xf * lax.rsqrt(jnp.mean(xf * xf, axis=-1, keepdims=True) + eps)
    return (y * g.astype(jnp.float32)).astype(x.dtype)


def centred_shift(p, mu):
    prev = jnp.pad(p[:, :-1], ((0, 0), (1, 0), (0, 0)))
    nxt = jnp.pad(p[:, 1:], ((0, 0), (0, 1), (0, 0)))
    return p + mu[0] * (prev - p) + mu[1] * (nxt - p)


def to_dirs(fwd, bwd):
    B, L, _ = fwd.shape
    t = jnp.stack([fwd, jnp.flip(bwd, axis=1)], axis=0).astype(jnp.float32)
    return t.reshape(N_DIR, B, L, RWKV_HEADS, HEAD_DIM).transpose(2, 0, 1, 3, 4)


def rwkv7_step(S, inp):
    r, w, k, v, kk, kka = inp
    sa = jnp.einsum('dbhij,dbhj->dbhi', S, -kk)
    S = S * w[..., None, :] + sa[..., :, None] * kka[..., None, :] + v[..., :, None] * k[..., None, :]
    y = jnp.einsum('dbhij,dbhj->dbhi', S, r)
    return S, y


def rwkv7_mixer(p, shift_mu, w0, w2, a0, a2, g2, k_k, k_a, r_k, lnx_g, lnx_b):
    B, L, _ = p.shape
    z = centred_shift(p, shift_mu)
    r, k, v, wd, ad, gd = jnp.split(z, RWKV_SPLITS, axis=-1)
    wd = wd.reshape(B, L, N_DIR, DECAY_LORA)
    ad = ad.reshape(B, L, N_DIR, ICL_LORA)
    w_log = -jax.nn.softplus(-(w0 + jnp.einsum('bldr,drc->bldc', jnp.tanh(wd), w2))) - 0.5
    decay = jnp.exp(-jnp.exp(w_log.astype(jnp.float32)))
    a = jax.nn.sigmoid(a0 + jnp.einsum('bldr,drc->bldc', ad, a2))
    g = jax.nn.sigmoid(gd) @ g2
    kk = (k * k_k).reshape(B, L, RWKV_HEADS, HEAD_DIM).astype(jnp.float32)
    kk = kk / jnp.maximum(jnp.linalg.norm(kk, axis=-1, keepdims=True), 1e-12)
    kk = kk.reshape(B, L, RWKV_WIDTH)
    k_dir = k[:, :, None, :] * (1.0 + (a - 1.0) * k_a)
    kka = kk[:, :, None, :] * a
    xs = (to_dirs(r, r), to_dirs(decay[:, :, 0], decay[:, :, 1]),
          to_dirs(k_dir[:, :, 0], k_dir[:, :, 1]), to_dirs(v, v), to_dirs(kk, kk),
          to_dirs(kka[:, :, 0], kka[:, :, 1]))
    S0 = jnp.zeros((N_DIR, B, RWKV_HEADS, HEAD_DIM, HEAD_DIM), jnp.float32)
    _, ys = lax.scan(rwkv7_step, S0, xs)
    ys = ys.transpose(1, 2, 0, 3, 4)
    y = ys[0] + jnp.flip(ys[1], axis=1)
    mu = jnp.mean(y, axis=-1, keepdims=True)
    var = jnp.mean(jnp.square(y - mu), axis=-1, keepdims=True)
    y = (y - mu) * lax.rsqrt(var + LNX_EPS)
    y = y * lnx_g.reshape(RWKV_HEADS, HEAD_DIM) + lnx_b.reshape(RWKV_HEADS, HEAD_DIM)
    rh = r.reshape(B, L, RWKV_HEADS, HEAD_DIM).astype(jnp.float32)
    kh = jnp.mean(k_dir, axis=2).reshape(B, L, RWKV_HEADS, HEAD_DIM).astype(jnp.float32)
    vh = v.reshape(B, L, RWKV_HEADS, HEAD_DIM).astype(jnp.float32)
    bonus = jnp.sum(rh * kh * r_k, axis=-1, keepdims=True) * vh
    out = (y + bonus).reshape(B, L, RWKV_WIDTH).astype(p.dtype)
    return out * g


def axial_rope_tables(n_tokens):
    rows = n_tokens // GRID_W
    inv_freq = ROPE_THETA ** (-jnp.arange(AXIS_FREQS, dtype=jnp.float32) * 2.0 / AXIS_DIM)
    row_ang = jnp.arange(rows, dtype=jnp.float32)[:, None] * inv_freq
    col_ang = jnp.arange(GRID_W, dtype=jnp.float32)[:, None] * inv_freq
    grid = jnp.stack([jnp.broadcast_to(row_ang[:, None, :], (rows, GRID_W, AXIS_FREQS)),
                      jnp.broadcast_to(col_ang[None, :, :], (rows, GRID_W, AXIS_FREQS))], axis=2)
    grid = grid.reshape(rows * GRID_W, 2, AXIS_FREQS)
    ang = jnp.concatenate([jnp.zeros((N_META, 2, AXIS_FREQS), jnp.float32), grid], axis=0)
    return jnp.cos(ang), jnp.sin(ang)


def apply_axial_rope(x, cos, sin):
    B, L, H, _ = x.shape
    xf = x.astype(jnp.float32).reshape(B, L, H, 2, 2, AXIS_FREQS)
    x1, x2 = xf[..., 0, :], xf[..., 1, :]
    c, s = cos[None, :, None], sin[None, :, None]
    out = jnp.stack([x1 * c - x2 * s, x2 * c + x1 * s], axis=-2)
    return out.reshape(B, L, H, HEAD_DIM).astype(x.dtype)


def gqa_axial_attention(q, k, v, q_norm_g, k_norm_g, cos, sin):
    B, L, _ = q.shape
    q = rms_norm(q.reshape(B, L, ATTN_Q_HEADS, HEAD_DIM), q_norm_g)
    k = rms_norm(k.reshape(B, L, ATTN_KV_HEADS, HEAD_DIM), k_norm_g)
    q = apply_axial_rope(q, cos, sin)
    k = apply_axial_rope(k, cos, sin)
    v = v.reshape(B, L, ATTN_KV_HEADS, HEAD_DIM)
    q = q.reshape(B, L, ATTN_KV_HEADS, ATTN_GROUP, HEAD_DIM) * (HEAD_DIM ** -0.5)

    def block_attn(qb):
        s = jnp.einsum('bqhgn,bkhn->bhgqk', qb, k).astype(jnp.float32)
        pr = jax.nn.softmax(s, axis=-1).astype(v.dtype)
        return jnp.einsum('bhgqk,bkhn->bqhgn', pr, v)

    meta_out = block_attn(q[:, :N_META])
    n_blk = (L - N_META) // Q_BLOCK
    qr = q[:, N_META:].reshape(B, n_blk, Q_BLOCK, ATTN_KV_HEADS, ATTN_GROUP, HEAD_DIM)
    real_out = lax.map(block_attn, qr.transpose(1, 0, 2, 3, 4, 5))
    real_out = real_out.transpose(1, 0, 2, 3, 4, 5).reshape(B, L - N_META, ATTN_Q_WIDTH)
    return jnp.concatenate([meta_out.reshape(B, N_META, ATTN_Q_WIDTH), real_out], axis=1)


def setup_inputs(seed: int = 0) -> dict:
    key = jax.random.key(seed)
    ks = jax.random.split(key, 26)
    f32 = jnp.float32

    def nrm(k, shape, scale):
        return jax.random.normal(k, shape, f32) * scale

    ramp = jnp.linspace(-6.5, -1.5, RWKV_WIDTH, dtype=f32)
    return {
        "x": nrm(ks[0], (BATCH, SEQ, D_MODEL), 1.0),
        "meta_tokens": nrm(ks[1], (N_META, D_MODEL), 1.0),
        "mix_norm_g": 1.0 + nrm(ks[2], (DEPTH, D_MODEL), 0.02),
        "w_in": nrm(ks[3], (DEPTH, D_MODEL, IN_WIDTH), D_MODEL ** -0.5),
        "rwkv_shift": jax.random.uniform(ks[4], (DEPTH, 2, RWKV_IN_WIDTH), f32, 0.0, 0.5),
        "decay_w0": ramp + nrm(ks[5], (DEPTH, N_DIR, RWKV_WIDTH), 0.1),
        "decay_w2": nrm(ks[6], (DEPTH, N_DIR, DECAY_LORA, RWKV_WIDTH), 0.5 * DECAY_LORA ** -0.5),
        "icl_a0": nrm(ks[7], (DEPTH, N_DIR, RWKV_WIDTH), 0.1),
        "icl_a2": nrm(ks[8], (DEPTH, N_DIR, ICL_LORA, RWKV_WIDTH), 0.5 * ICL_LORA ** -0.5),
        "gate_w2": nrm(ks[9], (DEPTH, GATE_LORA, RWKV_WIDTH), GATE_LORA ** -0.5),
        "k_k": 0.85 + nrm(ks[10], (DEPTH, RWKV_WIDTH), 0.02),
        "k_a": 1.0 + nrm(ks[11], (DEPTH, RWKV_WIDTH), 0.02),
        "r_k": nrm(ks[12], (DEPTH, RWKV_HEADS, HEAD_DIM), 0.1),
        "lnx_g": 1.0 + nrm(ks[13], (DEPTH, RWKV_WIDTH), 0.02),
        "lnx_b": nrm(ks[14], (DEPTH, RWKV_WIDTH), 0.02),
        "q_norm_g": 1.0 + nrm(ks[15], (DEPTH, HEAD_DIM), 0.02),
        "k_norm_g": 1.0 + nrm(ks[16], (DEPTH, HEAD_DIM), 0.02),
        "w_branch_rwkv": nrm(ks[17], (DEPTH, RWKV_WIDTH, D_MODEL), RWKV_WIDTH ** -0.5),
        "w_branch_attn": nrm(ks[18], (DEPTH, ATTN_Q_WIDTH, D_MODEL), ATTN_Q_WIDTH ** -0.5),
        "w_out": nrm(ks[19], (DEPTH, D_MODEL, D_MODEL), D_MODEL ** -0.5),
        "ffn_norm_g": 1.0 + nrm(ks[20], (DEPTH, D_MODEL), 0.02),
        "w_ff1": nrm(ks[21], (DEPTH, D_MODEL, D_FF), D_MODEL ** -0.5),
        "w_ff2": nrm(ks[22], (DEPTH, D_FF, D_MODEL), D_FF ** -0.5),
        "final_norm_g": 1.0 + nrm(ks[23], (D_MODEL,), 0.02),
    }


def reference(x, meta_tokens, mix_norm_g, w_in, rwkv_shift, decay_w0, decay_w2, icl_a0, icl_a2,
              gate_w2, k_k, k_a, r_k, lnx_g, lnx_b, q_norm_g, k_norm_g, w_branch_rwkv,
              w_branch_attn, w_out, ffn_norm_g, w_ff1, w_ff2, final_norm_g):
    B, n_tok, D = x.shape
    meta = jnp.broadcast_to(meta_tokens[None].astype(x.dtype), (B, N_META, D))
    h = jnp.concatenate([meta, x], axis=1)
    cos, sin = axial_rope_tables(n_tok)
    for i in range(DEPTH):
        u = rms_norm(h, mix_norm_g[i])
        proj = u @ w_in[i]
        p_rwkv, p_q, p_k, p_v, p_gate_a, p_gate_b = jnp.split(proj, IN_SPLITS, axis=-1)
        y_a = rwkv7_mixer(p_rwkv, rwkv_shift[i], decay_w0[i], decay_w2[i], icl_a0[i], icl_a2[i],
                          gate_w2[i], k_k[i], k_a[i], r_k[i], lnx_g[i], lnx_b[i])
        y_b = gqa_axial_attention(p_q, p_k, p_v, q_norm_g[i], k_norm_g[i], cos, sin)
        merged = (jax.nn.sigmoid(p_gate_a) * (y_a @ w_branch_rwkv[i])
                  + jax.nn.sigmoid(p_gate_b) * (y_b @ w_branch_attn[i]))
        h = h + merged @ w_out[i]
        f = rms_norm(h, ffn_norm_g[i]) @ w_ff1[i]
        h = h + jnp.square(jax.nn.relu(f)) @ w_ff2[i]
    return rms_norm(h, final_norm_g)[:, N_META:]
```

```python
import functools

import jax
import jax.numpy as jnp
from jax import lax
from jax.experimental import pallas as pl
from jax.experimental.pallas import tpu as pltpu

f32 = jnp.float32
bf16 = jnp.bfloat16

N_META = 16
GRID_W = 64
HEAD_DIM = 64
RWKV_HEADS = 8
RWKV_WIDTH = RWKV_HEADS * HEAD_DIM
DECAY_LORA = 64
ICL_LORA = 64
GATE_LORA = 160
LNX_EPS = 64e-5
ATTN_Q_HEADS = 8
ATTN_KV_HEADS = 2
ATTN_GROUP = ATTN_Q_HEADS // ATTN_KV_HEADS
ATTN_Q_WIDTH = ATTN_Q_HEADS * HEAD_DIM
ATTN_KV_WIDTH = ATTN_KV_HEADS * HEAD_DIM
ROPE_THETA = 10000.0
AXIS_DIM = HEAD_DIM // 2
AXIS_FREQS = AXIS_DIM // 2
NORM_EPS = 1e-6

LANES = 128
CHUNK = 64
FRONT_PAD = CHUNK - N_META
HALO = 8
PAIR = 2 * HEAD_DIM
N_PAIRS = RWKV_WIDTH // PAIR
MASK_BIAS = -1e30
VMEM_LIMIT = 56 * 1024 * 1024

RWKV_IN = 3 * RWKV_WIDTH + 2 * DECAY_LORA + 2 * ICL_LORA + GATE_LORA
RWKV_IN_PAD = 2048
COL_GATE_A = 0
COL_GATE_B = 1024
COL_RWKV = 2048
COL_Q = 4096
COL_K = 4608
COL_V = 4736
PROJ_W = 4864
OFF_R, OFF_K, OFF_V = 0, RWKV_WIDTH, 2 * RWKV_WIDTH
OFF_WD = 3 * RWKV_WIDTH
OFF_AD = OFF_WD + 2 * DECAY_LORA
OFF_GD = OFF_AD + 2 * ICL_LORA
GD_PAD = RWKV_IN_PAD - OFF_GD


def _pick_tile(n, candidates):
    for c in candidates:
        if n % c == 0:
            return c
    raise ValueError(f"no tile for {n} in {candidates}")


def _split3(x):
    h = x.astype(bf16)
    r = x - h.astype(f32)
    m = r.astype(bf16)
    l = (r - m.astype(f32)).astype(bf16)
    return h, m, l


def _dot(a, b, dims=None):
    if dims is None:
        return jnp.dot(a, b, preferred_element_type=f32)
    return lax.dot_general(a, b, (dims, ((), ())), preferred_element_type=f32)


def _mm(a, b, passes=1, dims=None):
    if passes == 1:
        return _dot(a.astype(bf16), b.astype(bf16), dims)
    ah, am, al = _split3(a)
    bh, bm, bl = _split3(b)
    out = _dot(ah, bh, dims) + (_dot(ah, bm, dims) + _dot(am, bh, dims))
    if passes == 3:
        return out
    return out + (_dot(am, bm, dims) + _dot(ah, bl, dims) + _dot(al, bh, dims))


def _mm_exact_lhs(a_bf16, b, dims=None):
    bh, bm, bl = _split3(b)
    return _dot(a_bf16, bh, dims) + _dot(a_bf16, bm, dims) + _dot(a_bf16, bl, dims)


def _mm_exact_rhs(a, b_bf16):
    ah, am, al = _split3(a)
    return _dot(ah, b_bf16) + _dot(am, b_bf16) + _dot(al, b_bf16)


def _head_sum_matrix(scale):
    r = lax.broadcasted_iota(jnp.int32, (LANES, LANES), 0) // HEAD_DIM
    c = lax.broadcasted_iota(jnp.int32, (LANES, LANES), 1) // HEAD_DIM
    return jnp.where(r == c, scale, 0.0).astype(bf16)


def _head_sums(x, g):
    w = x.shape[-1]
    parts = [_mm_exact_rhs(x[:, i:i + LANES], g) for i in range(0, w, LANES)]
    return parts[0] if len(parts) == 1 else jnp.concatenate(parts, axis=-1)


def _in_proj_kernel(h_ref, g_ref, w_ref, o_ref, u_ref):
    @pl.when(pl.program_id(1) == 0)
    def _():
        x = h_ref[...]
        ms = jnp.mean(x * x, axis=-1, keepdims=True)
        u_ref[...] = (x * lax.rsqrt(ms + NORM_EPS) * g_ref[...]).astype(bf16)

    o_ref[...] = jnp.dot(u_ref[...], w_ref[...], preferred_element_type=f32)


def _in_proj(h2d, g, w):
    n, d = h2d.shape
    tm = _pick_tile(n, (640, 320, 128, 64))
    tn = PROJ_W // 2
    return pl.pallas_call(
        _in_proj_kernel,
        grid=(n // tm, PROJ_W // tn),
        in_specs=[pl.BlockSpec((tm, d), lambda i, j: (i, 0)),
                  pl.BlockSpec((1, d), lambda i, j: (0, 0)),
                  pl.BlockSpec((d, tn), lambda i, j: (0, j))],
        out_specs=pl.BlockSpec((tm, tn), lambda i, j: (i, j)),
        out_shape=jax.ShapeDtypeStruct((n, PROJ_W), f32),
        scratch_shapes=[pltpu.VMEM((tm, d), bf16)],
        compiler_params=pltpu.CompilerParams(
            dimension_semantics=("parallel", "arbitrary"), vmem_limit_bytes=VMEM_LIMIT),
        name="in_proj",
    )(h2d, g, w)


def _norm_rope(x, gain, cos, sin_signed, gsum, scale):
    ms = _mm_exact_rhs(x * x, gsum)
    xn = x * lax.rsqrt(ms + NORM_EPS) * gain
    lane = lax.broadcasted_iota(jnp.int32, x.shape, 1)
    first_half = (lane % AXIS_DIM) < AXIS_FREQS
    partner = jnp.where(first_half,
                        pltpu.roll(xn, LANES - AXIS_FREQS, axis=1),
                        pltpu.roll(xn, AXIS_FREQS, axis=1))
    out = xn * cos + partner * sin_signed
    return out * scale if scale != 1.0 else out


def _qkv_kernel(q_ref, k_ref, v_ref, cos_ref, sin_ref, qg_ref, kg_ref, qo_ref, ko_ref, vo_ref):
    gsum = _head_sum_matrix(1.0 / HEAD_DIM)
    cos = cos_ref[...]
    sin = sin_ref[...]
    for i in range(0, ATTN_Q_WIDTH, LANES):
        qo_ref[:, i:i + LANES] = _norm_rope(q_ref[:, i:i + LANES], qg_ref[...], cos, sin, gsum,
                                            HEAD_DIM ** -0.5).astype(bf16)
    ko_ref[...] = _norm_rope(k_ref[...], kg_ref[...], cos, sin, gsum, 1.0).astype(bf16)
    vo_ref[...] = v_ref[...].astype(bf16)


def _qkv_prep(proj3, cos, sin, qg, kg):
    b, lp, _ = proj3.shape
    t = _pick_tile(lp, (1040, 832, 320, 64))
    row = lambda w, cb: pl.BlockSpec((None, t, w), lambda bi, ti: (bi, ti, cb))
    tab = pl.BlockSpec((t, LANES), lambda bi, ti: (ti, 0))
    vec = pl.BlockSpec((1, LANES), lambda bi, ti: (0, 0))
    out = lambda w: pl.BlockSpec((None, t, w), lambda bi, ti: (bi, ti, 0))
    return pl.pallas_call(
        _qkv_kernel,
        grid=(b, lp // t),
        in_specs=[row(ATTN_Q_WIDTH, COL_Q // ATTN_Q_WIDTH),
                  row(ATTN_KV_WIDTH, COL_K // ATTN_KV_WIDTH),
                  row(ATTN_KV_WIDTH, COL_V // ATTN_KV_WIDTH),
                  tab, tab, vec, vec],
        out_specs=[out(ATTN_Q_WIDTH), out(ATTN_KV_WIDTH), out(ATTN_KV_WIDTH)],
        out_shape=[jax.ShapeDtypeStruct((b, lp, ATTN_Q_WIDTH), bf16),
                   jax.ShapeDtypeStruct((b, lp, ATTN_KV_WIDTH), bf16),
                   jax.ShapeDtypeStruct((b, lp, ATTN_KV_WIDTH), bf16)],
        compiler_params=pltpu.CompilerParams(dimension_semantics=("parallel", "parallel")),
        name="qkv_prep",
    )(proj3, proj3, proj3, cos, sin, qg, kg)


def _attn_kernel(q_ref, k_ref, v_ref, bias_ref, o_ref):
    g, tq, n = q_ref.shape
    q = q_ref[...].reshape(g * tq, n)
    s = lax.dot_general(q, k_ref[...], (((1,), (1,)), ((), ())), preferred_element_type=f32)
    s = s + bias_ref[...]
    m = jnp.max(s, axis=-1, keepdims=True)
    p = jnp.exp(s - m)
    l = jnp.sum(p, axis=-1, keepdims=True)
    o = jnp.dot(p.astype(bf16), v_ref[...], preferred_element_type=f32) / l
    o_ref[...] = o.reshape(g, tq, n)


def _attention(q5, k4, v4, bias):
    b, kvh, g, lp, n = q5.shape
    lk = k4.shape[2]
    tq = CHUNK
    return pl.pallas_call(
        _attn_kernel,
        grid=(b, kvh, lp // tq),
        in_specs=[pl.BlockSpec((None, None, g, tq, n), lambda bi, hi, qi: (bi, hi, 0, qi, 0)),
                  pl.BlockSpec((None, None, lk, n), lambda bi, hi, qi: (bi, hi, 0, 0)),
                  pl.BlockSpec((None, None, lk, n), lambda bi, hi, qi: (bi, hi, 0, 0)),
                  pl.BlockSpec((1, lk), lambda bi, hi, qi: (0, 0))],
        out_specs=pl.BlockSpec((None, None, g, tq, n), lambda bi, hi, qi: (bi, hi, 0, qi, 0)),
        out_shape=jax.ShapeDtypeStruct((b, kvh, g, lp, n), f32),
        compiler_params=pltpu.CompilerParams(
            dimension_semantics=("parallel", "parallel", "parallel"), vmem_limit_bytes=VMEM_LIMIT),
        name="attention",
    )(q5, k4, v4, bias)


def _shifted_rows(p, prev_row, next_row):
    rows = lax.broadcasted_iota(jnp.int32, p.shape, 0)
    n = p.shape[0]
    prev = jnp.where(rows == 0, prev_row, pltpu.roll(p, 1, axis=0))
    nxt = jnp.where(rows == n - 1, next_row, pltpu.roll(p, n - 1, axis=0))
    return prev, nxt


def _stack_pair(x):
    lane = lax.broadcasted_iota(jnp.int32, x.shape, 1)
    first = lane < HEAD_DIM
    return jnp.concatenate([jnp.where(first, x, 0.0), jnp.where(first, 0.0, x)], axis=0)


def _chunk_scan(direction, p, prev_row, next_row, chunk_idx, prm, h_ref, y_ref, gate_ref, bonus_ref):
    (mu_ref, w0_ref, w2_ref, a0_ref, a2_ref, g2_ref, kk_ref, ka_ref, rk_ref) = prm
    mu = mu_ref[...]
    prev, nxt = _shifted_rows(p, prev_row, next_row)
    z = p + mu[0:1] * (prev - p) + mu[1:2] * (nxt - p)

    rows = lax.broadcasted_iota(jnp.int32, (CHUNK, 1), 0) + chunk_idx * CHUNK
    valid = rows >= FRONT_PAD
    z = jnp.where(valid, z, 0.0)

    r = z[:, OFF_R:OFF_R + RWKV_WIDTH]
    k = z[:, OFF_K:OFF_K + RWKV_WIDTH]
    v = z[:, OFF_V:OFF_V + RWKV_WIDTH]
    lo, hi = direction * RWKV_WIDTH, (direction + 1) * RWKV_WIDTH

    x_w = w0_ref[...] + _mm(jnp.tanh(z[:, OFF_WD:OFF_AD]), w2_ref[...], 3)
    logw = -jnp.exp(f32(-0.5)) * jax.nn.sigmoid(x_w[:, lo:hi])
    logw = jnp.where(valid, logw, 0.0)
    a_all = jax.nn.sigmoid(a0_ref[...] + _mm(z[:, OFF_AD:OFF_GD], a2_ref[...], 3))
    a_dir = a_all[:, lo:hi]

    gsum = _head_sum_matrix(1.0)
    kk = k * kk_ref[...]
    kk = kk / jnp.maximum(jnp.sqrt(_head_sums(kk * kk, gsum)), 1e-12)
    k_a = ka_ref[...]
    k_dir = k * (1.0 + (a_dir - 1.0) * k_a)
    kka = kk * a_dir

    if direction == 0:
        gate_ref[...] = _mm(jax.nn.sigmoid(z[:, OFF_GD:RWKV_IN_PAD]), g2_ref[...], 3)
        a_mean = 0.5 * (a_all[:, :RWKV_WIDTH] + a_all[:, RWKV_WIDTH:])
        k_mean = k * (1.0 + (a_mean - 1.0) * k_a)
        bonus_ref[...] = _head_sums(r * k_mean * rk_ref[...], gsum) * v

    tr = lax.broadcasted_iota(jnp.int32, (CHUNK, CHUNK), 0)
    tc = lax.broadcasted_iota(jnp.int32, (CHUNK, CHUNK), 1)
    tri = (tc <= tr) if direction == 0 else (tc >= tr)
    lc = _mm_exact_lhs(jnp.where(tri, 1.0, 0.0).astype(bf16), logw)
    last = CHUNK - 1 if direction == 0 else 0
    ltot = lc[last:last + 1, :]
    e_in = jnp.exp(lc)
    e_ex = jnp.exp(lc - logw)
    e_neg = jnp.exp(-lc)
    e_end = jnp.exp(ltot - lc)
    r_t = r * e_in
    a_t = -kk * e_ex
    b_t = kka * e_neg
    k_t = k_dir * e_neg
    b_e = kka * e_end
    k_e = k_dir * e_end
    e_tot = jnp.exp(ltot)

    sr = lax.broadcasted_iota(jnp.int32, (PAIR, PAIR), 0)
    sc = lax.broadcasted_iota(jnp.int32, (PAIR, PAIR), 1)
    same = (sr // CHUNK) == (sc // CHUNK)
    tt, ss = sr % CHUNK, sc % CHUNK
    strict = same & ((ss < tt) if direction == 0 else (ss > tt))
    incl = same & ((ss <= tt) if direction == 0 else (ss >= tt))
    eye = sr == sc

    nt = ((1,), (1,))
    tn = ((0,), (0,))
    for j in range(N_PAIRS):
        sl = slice(j * PAIR, (j + 1) * PAIR)
        a_s, r_s = _stack_pair(a_t[:, sl]), _stack_pair(r_t[:, sl])
        b_s, k_s = _stack_pair(b_t[:, sl]), _stack_pair(k_t[:, sl])
        v_s = _stack_pair(v[:, sl])
        be_s, ke_s = _stack_pair(b_e[:, sl]), _stack_pair(k_e[:, sl])

        a_ab = jnp.where(strict, _mm(a_s, b_s, 3, nt), 0.0)
        a_ak = jnp.where(strict, _mm(a_s, k_s, 3, nt), 0.0)
        a_rb = jnp.where(incl, _mm(r_s, b_s, 3, nt), 0.0)
        a_rk = jnp.where(incl, _mm(r_s, k_s, 3, nt), 0.0)

        t_inv = jnp.where(eye, 1.0, 0.0) + a_ab
        pw = a_ab
        for _ in range(5):
            pw = _mm(pw, pw, 3)
            t_inv = t_inv + _mm(t_inv, pw, 3)

        av = _mm(a_ak, v_s, 3)
        a2 = _mm(t_inv, a_s, 3)
        v2 = _mm(t_inv, av, 3)
        q_s = r_s + _mm(a_rb, a2, 3)
        y0 = _mm(a_rb, v2, 3) + _mm(a_rk, v_s, 3)

        h0 = h_ref[direction, j]
        y_st = _mm(q_s, h0, 3) + y0
        y_ref[:, sl] = y_st[:CHUNK] + y_st[CHUNK:]

        m_lr = _mm(be_s, a2, 3, tn)
        n_new = _mm(be_s, v2, 3, tn) + _mm(ke_s, v_s, 3, tn)
        decay_col = jnp.sum(jnp.where(eye, e_tot[:, sl], 0.0), axis=1, keepdims=True)
        h_ref[direction, j] = decay_col * h0 + _mm(m_lr, h0, 3) + n_new


def _rwkv_kernel(pf_ref, pf_prev_ref, pf_next_ref, pb_ref, pb_prev_ref, pb_next_ref,
                 mu_ref, w0_ref, w2_ref, a0_ref, a2_ref, g2_ref, kk_ref, ka_ref, rk_ref,
                 yf_ref, yb_ref, gate_ref, bonus_ref, h_ref, *, n_chunks):
    c = pl.program_id(1)

    @pl.when(c == 0)
    def _():
        h_ref[...] = jnp.zeros_like(h_ref)

    prm = (mu_ref, w0_ref, w2_ref, a0_ref, a2_ref, g2_ref, kk_ref, ka_ref, rk_ref)
    for direction, (p_ref, prev_ref, next_ref, y_ref) in enumerate(
            ((pf_ref, pf_prev_ref, pf_next_ref, yf_ref), (pb_ref, pb_prev_ref, pb_next_ref, yb_ref))):
        ci = c if direction == 0 else n_chunks - 1 - c
        prev_row = jnp.where(ci == 0, 0.0, prev_ref[HALO - 1:HALO, :])
        next_row = jnp.where(ci == n_chunks - 1, 0.0, next_ref[0:1, :])
        _chunk_scan(direction, p_ref[...], prev_row, next_row, ci, prm, h_ref, y_ref,
                    gate_ref, bonus_ref)


def _rwkv(proj3, mu, w0, w2, a0, a2, g2, k_k, k_a, r_k):
    b, lp, _ = proj3.shape
    nc = lp // CHUNK
    hb = CHUNK // HALO
    cb = COL_RWKV // RWKV_IN_PAD
    fwd = lambda bi, c: (bi, c, cb)
    bwd = lambda bi, c: (bi, nc - 1 - c, cb)
    fwd_prev = lambda bi, c: (bi, jnp.maximum(c * hb - 1, 0), cb)
    fwd_next = lambda bi, c: (bi, jnp.minimum((c + 1) * hb, nc * hb - 1), cb)
    bwd_prev = lambda bi, c: (bi, jnp.maximum((nc - 1 - c) * hb - 1, 0), cb)
    bwd_next = lambda bi, c: (bi, jnp.minimum((nc - c) * hb, nc * hb - 1), cb)
    main = lambda im: pl.BlockSpec((None, CHUNK, RWKV_IN_PAD), im)
    halo = lambda im: pl.BlockSpec((None, HALO, RWKV_IN_PAD), im)
    full = lambda a: pl.BlockSpec(a.shape, lambda bi, c: (0,) * a.ndim)
    out = lambda im: pl.BlockSpec((None, CHUNK, RWKV_WIDTH), im)
    o_fwd = lambda bi, c: (bi, c, 0)
    o_bwd = lambda bi, c: (bi, nc - 1 - c, 0)
    shp = jax.ShapeDtypeStruct((b, lp, RWKV_WIDTH), f32)
    params = (mu, w0, w2, a0, a2, g2, k_k, k_a, r_k)
    return pl.pallas_call(
        functools.partial(_rwkv_kernel, n_chunks=nc),
        grid=(b, nc),
        in_specs=[main(fwd), halo(fwd_prev), halo(fwd_next),
                  main(bwd), halo(bwd_prev), halo(bwd_next)] + [full(a) for a in params],
        out_specs=[out(o_fwd), out(o_bwd), out(o_fwd), out(o_fwd)],
        out_shape=[shp, shp, shp, shp],
        scratch_shapes=[pltpu.VMEM((2, N_PAIRS, PAIR, PAIR), f32)],
        compiler_params=pltpu.CompilerParams(
            dimension_semantics=("parallel", "arbitrary"), vmem_limit_bytes=VMEM_LIMIT),
        name="rwkv",
    )(proj3, proj3, proj3, proj3, proj3, proj3, *params)


def _post_kernel(yf_ref, yb_ref, gate_ref, bonus_ref, att_ref, ga_ref, gb_ref, h_ref,
                 lng_ref, lnb_ref, wa_ref, wb_ref, wo_ref, o_ref):
    gmean = _head_sum_matrix(1.0 / HEAD_DIM)
    y = yf_ref[...] + yb_ref[...]
    mu = _head_sums(y, gmean)
    d = y - mu
    var = _head_sums(d * d, gmean)
    yn = d * lax.rsqrt(var + LNX_EPS) * lng_ref[...] + lnb_ref[...]
    out_a = ((yn + bonus_ref[...]) * gate_ref[...]).astype(bf16)
    ya = jnp.dot(out_a, wa_ref[...], preferred_element_type=f32)
    yb = jnp.dot(att_ref[...].astype(bf16), wb_ref[...], preferred_element_type=f32)
    merged = jax.nn.sigmoid(ga_ref[...]) * ya + jax.nn.sigmoid(gb_ref[...]) * yb
    o_ref[...] = h_ref[...] + jnp.dot(merged.astype(bf16), wo_ref[...], preferred_element_type=f32)


def _post(yf, yb, gate, bonus, att, proj, h2d, lng, lnb, wa, wb, wo):
    n, d = h2d.shape
    tm = _pick_tile(n, (640, 320, 128, 64))
    row = lambda w, cb=0: pl.BlockSpec((tm, w), lambda i: (i, cb))
    full = lambda a: pl.BlockSpec(a.shape, lambda i: (0,) * a.ndim)
    return pl.pallas_call(
        _post_kernel,
        grid=(n // tm,),
        in_specs=[row(RWKV_WIDTH), row(RWKV_WIDTH), row(RWKV_WIDTH), row(RWKV_WIDTH),
                  row(ATTN_Q_WIDTH), row(d, COL_GATE_A // d), row(d, COL_GATE_B // d), row(d),
                  full(lng), full(lnb), full(wa), full(wb), full(wo)],
        out_specs=row(d),
        out_shape=jax.ShapeDtypeStruct((n, d), f32),
        compiler_params=pltpu.CompilerParams(
            dimension_semantics=("parallel",), vmem_limit_bytes=VMEM_LIMIT),
        name="post",
    )(yf, yb, gate, bonus, att, proj, proj, h2d, lng, lnb, wa, wb, wo)


def _ffn_kernel(h_ref, g_ref, w1_ref, w2_ref, gf_ref, o_ref, *, ff_chunk):
    h = h_ref[...]
    ms = jnp.mean(h * h, axis=-1, keepdims=True)
    xn = (h * lax.rsqrt(ms + NORM_EPS) * g_ref[...]).astype(bf16)
    acc = h
    for c in range(0, w1_ref.shape[1], ff_chunk):
        f = jnp.dot(xn, w1_ref[:, c:c + ff_chunk], preferred_element_type=f32)
        f = jnp.square(jnp.maximum(f, 0.0)).astype(bf16)
        acc = acc + jnp.dot(f, w2_ref[c:c + ff_chunk, :], preferred_element_type=f32)
    ms2 = jnp.mean(acc * acc, axis=-1, keepdims=True)
    o_ref[...] = acc * lax.rsqrt(ms2 + NORM_EPS) * gf_ref[...]


def _ffn(h2d, g, w1, w2, gf):
    n, d = h2d.shape
    tm = _pick_tile(n, (640, 320, 128, 64))
    row = pl.BlockSpec((tm, d), lambda i: (i, 0))
    full = lambda a: pl.BlockSpec(a.shape, lambda i: (0,) * a.ndim, pipeline_mode=pl.Buffered(1))
    return pl.pallas_call(
        functools.partial(_ffn_kernel, ff_chunk=512),
        grid=(n // tm,),
        in_specs=[row, full(g), full(w1), full(w2), full(gf)],
        out_specs=row,
        out_shape=jax.ShapeDtypeStruct((n, d), f32),
        compiler_params=pltpu.CompilerParams(
            dimension_semantics=("parallel",), vmem_limit_bytes=VMEM_LIMIT),
        name="ffn",
    )(h2d, g, w1, w2, gf)


def _rope_tables(n_tok, lp):
    rows = n_tok // GRID_W
    inv_freq = ROPE_THETA ** (-jnp.arange(AXIS_FREQS, dtype=f32) * 2.0 / AXIS_DIM)
    row_ang = jnp.arange(rows, dtype=f32)[:, None] * inv_freq
    col_ang = jnp.arange(GRID_W, dtype=f32)[:, None] * inv_freq
    grid = jnp.stack([jnp.broadcast_to(row_ang[:, None, :], (rows, GRID_W, AXIS_FREQS)),
                      jnp.broadcast_to(col_ang[None, :, :], (rows, GRID_W, AXIS_FREQS))], axis=2)
    grid = grid.reshape(rows * GRID_W, 2, AXIS_FREQS)
    ang = jnp.concatenate([jnp.zeros((lp - n_tok, 2, AXIS_FREQS), f32), grid], axis=0)
    cos, sin = jnp.cos(ang), jnp.sin(ang)
    cos64 = jnp.stack([cos, cos], axis=2).reshape(lp, HEAD_DIM)
    sin64 = jnp.stack([-sin, sin], axis=2).reshape(lp, HEAD_DIM)
    return jnp.tile(cos64, (1, 2)), jnp.tile(sin64, (1, 2))


def kernel(x, meta_tokens, mix_norm_g, w_in, rwkv_shift, decay_w0, decay_w2, icl_a0, icl_a2, gate_w2, k_k, k_a, r_k, lnx_g, lnx_b, q_norm_g, k_norm_g, w_branch_rwkv, w_branch_attn, w_out, ffn_norm_g, w_ff1, w_ff2, final_norm_g):
    b, n_tok, d = x.shape
    lp = CHUNK + n_tok
    n = b * lp

    meta = jnp.broadcast_to(meta_tokens[None].astype(x.dtype), (b, N_META, d))
    h = jnp.concatenate([jnp.zeros((b, FRONT_PAD, d), x.dtype), meta, x], axis=1)
    h2d = h.reshape(n, d)

    w = w_in[0]
    s0 = RWKV_IN
    s1 = s0 + ATTN_Q_WIDTH
    s2 = s1 + ATTN_KV_WIDTH
    s3 = s2 + ATTN_KV_WIDTH
    s4 = s3 + d
    w_perm = jnp.concatenate(
        [w[:, s3:s4], w[:, s4:], w[:, :s0], jnp.zeros((d, RWKV_IN_PAD - RWKV_IN), w.dtype),
         w[:, s0:s1], w[:, s1:s2], w[:, s2:s3]], axis=1).astype(bf16)
    mu = jnp.pad(rwkv_shift[0], ((0, 0), (0, RWKV_IN_PAD - RWKV_IN)))
    zl = jnp.zeros((DECAY_LORA, RWKV_WIDTH), f32)
    w2cat = jnp.concatenate([jnp.concatenate([decay_w2[0, 0], zl], axis=1),
                             jnp.concatenate([zl, decay_w2[0, 1]], axis=1)], axis=0)
    a2cat = jnp.concatenate([jnp.concatenate([icl_a2[0, 0], zl], axis=1),
                             jnp.concatenate([zl, icl_a2[0, 1]], axis=1)], axis=0)
    w0cat = decay_w0[0].reshape(1, 2 * RWKV_WIDTH)
    a0cat = icl_a0[0].reshape(1, 2 * RWKV_WIDTH)
    g2pad = jnp.pad(gate_w2[0], ((0, GD_PAD - GATE_LORA), (0, 0)))
    row = lambda a: a.reshape(1, -1)

    proj = _in_proj(h2d, row(mix_norm_g[0]), w_perm)
    proj3 = proj.reshape(b, lp, PROJ_W)

    cos, sin = _rope_tables(n_tok, lp)
    qg = jnp.tile(row(q_norm_g[0]), (1, 2))
    kg = jnp.tile(row(k_norm_g[0]), (1, 2))
    q, k, v = _qkv_prep(proj3, cos, sin, qg, kg)
    lk = -(-lp // LANES) * LANES
    q5 = q.reshape(b, lp, ATTN_KV_HEADS, ATTN_GROUP, HEAD_DIM).transpose(0, 2, 3, 1, 4)
    kv_layout = lambda t: jnp.pad(
        t.reshape(b, lp, ATTN_KV_HEADS, HEAD_DIM).transpose(0, 2, 1, 3),
        ((0, 0), (0, 0), (0, lk - lp), (0, 0)))
    pos = jnp.arange(lk)
    bias = jnp.where((pos >= FRONT_PAD) & (pos < lp), 0.0, MASK_BIAS).astype(f32)[None]
    att = _attention(q5, kv_layout(k), kv_layout(v), bias)
    att = att.transpose(0, 3, 1, 2, 4).reshape(n, ATTN_Q_WIDTH)

    yf, yb, gate, bonus = _rwkv(proj3, mu, w0cat, w2cat, a0cat, a2cat, g2pad,
                                row(k_k[0]), row(k_a[0]), row(r_k[0]))
    flat = lambda t: t.reshape(n, RWKV_WIDTH)

    h1 = _post(flat(yf), flat(yb), flat(gate), flat(bonus), att, proj, h2d,
               row(lnx_g[0]), row(lnx_b[0]), w_branch_rwkv[0].astype(bf16),
               w_branch_attn[0].astype(bf16), w_out[0].astype(bf16))
    out = _ffn(h1, row(ffn_norm_g[0]), w_ff1[0].astype(bf16), w_ff2[0].astype(bf16),
               row(final_norm_g))
    return out.reshape(b, lp, d)[:, CHUNK:]
```

```python
import functools

import jax
import jax.numpy as jnp
from jax import lax
from jax.experimental import pallas as pl
from jax.experimental.pallas import tpu as pltpu

f32 = jnp.float32
bf16 = jnp.bfloat16

N_META = 16
GRID_W = 64
HEAD_DIM = 64
RWKV_HEADS = 8
RWKV_WIDTH = RWKV_HEADS * HEAD_DIM
DECAY_LORA = 64
ICL_LORA = 64
GATE_LORA = 160
LNX_EPS = 64e-5
ATTN_Q_HEADS = 8
ATTN_KV_HEADS = 2
ATTN_GROUP = ATTN_Q_HEADS // ATTN_KV_HEADS
ATTN_Q_WIDTH = ATTN_Q_HEADS * HEAD_DIM
ATTN_KV_WIDTH = ATTN_KV_HEADS * HEAD_DIM
ROPE_THETA = 10000.0
AXIS_DIM = HEAD_DIM // 2
AXIS_FREQS = AXIS_DIM // 2
NORM_EPS = 1e-6

LANES = 128
CHUNK = 64
FRONT_PAD = CHUNK - N_META
HALO = 8
PAIR = 2 * HEAD_DIM
N_PAIRS = RWKV_WIDTH // PAIR
MASK_BIAS = -1e30
VMEM_LIMIT = 56 * 1024 * 1024

RWKV_IN = 3 * RWKV_WIDTH + 2 * DECAY_LORA + 2 * ICL_LORA + GATE_LORA
RWKV_IN_PAD = 2048
COL_GATE_A = 0
COL_GATE_B = 1024
COL_RWKV = 2048
COL_Q = 4096
COL_K = 4608
COL_V = 4736
PROJ_W = 4864
OFF_R, OFF_K, OFF_V = 0, RWKV_WIDTH, 2 * RWKV_WIDTH
OFF_WD = 3 * RWKV_WIDTH
OFF_AD = OFF_WD + 2 * DECAY_LORA
OFF_GD = OFF_AD + 2 * ICL_LORA
GD_PAD = RWKV_IN_PAD - OFF_GD


def _pick_tile(n, candidates):
    for c in candidates:
        if n % c == 0:
            return c
    raise ValueError(f"no tile for {n} in {candidates}")


def _split3(x):
    h = x.astype(bf16)
    r = x - h.astype(f32)
    m = r.astype(bf16)
    l = (r - m.astype(f32)).astype(bf16)
    return h, m, l


def _dot(a, b, dims=None):
    if dims is None:
        return jnp.dot(a, b, preferred_element_type=f32)
    return lax.dot_general(a, b, (dims, ((), ())), preferred_element_type=f32)


def _mm(a, b, passes=1, dims=None):
    if passes == 1:
        return _dot(a.astype(bf16), b.astype(bf16), dims)
    ah, am, al = _split3(a)
    bh, bm, bl = _split3(b)
    out = _dot(ah, bh, dims) + (_dot(ah, bm, dims) + _dot(am, bh, dims))
    if passes == 3:
        return out
    return out + (_dot(am, bm, dims) + _dot(ah, bl, dims) + _dot(al, bh, dims))


def _mm_exact_lhs(a_bf16, b, dims=None):
    bh, bm, bl = _split3(b)
    return _dot(a_bf16, bh, dims) + _dot(a_bf16, bm, dims) + _dot(a_bf16, bl, dims)


def _mm_exact_rhs(a, b_bf16):
    ah, am, al = _split3(a)
    return _dot(ah, b_bf16) + _dot(am, b_bf16) + _dot(al, b_bf16)


def _head_sum_matrix(scale):
    r = lax.broadcasted_iota(jnp.int32, (LANES, LANES), 0) // HEAD_DIM
    c = lax.broadcasted_iota(jnp.int32, (LANES, LANES), 1) // HEAD_DIM
    return jnp.where(r == c, scale, 0.0).astype(bf16)


def _head_sums(x, g):
    w = x.shape[-1]
    parts = [_mm_exact_rhs(x[:, i:i + LANES], g) for i in range(0, w, LANES)]
    return parts[0] if len(parts) == 1 else jnp.concatenate(parts, axis=-1)


def _in_proj_kernel(h_ref, g_ref, w_ref, o_ref, u_ref):
    @pl.when(pl.program_id(1) == 0)
    def _():
        x = h_ref[...]
        ms = jnp.mean(x * x, axis=-1, keepdims=True)
        u_ref[...] = (x * lax.rsqrt(ms + NORM_EPS) * g_ref[...]).astype(bf16)

    o_ref[...] = jnp.dot(u_ref[...], w_ref[...], preferred_element_type=f32)


def _in_proj(h2d, g, w):
    n, d = h2d.shape
    tm = _pick_tile(n, (640, 320, 128, 64))
    tn = PROJ_W // 2
    return pl.pallas_call(
        _in_proj_kernel,
        grid=(n // tm, PROJ_W // tn),
        in_specs=[pl.BlockSpec((tm, d), lambda i, j: (i, 0)),
                  pl.BlockSpec((1, d), lambda i, j: (0, 0)),
                  pl.BlockSpec((d, tn), lambda i, j: (0, j))],
        out_specs=pl.BlockSpec((tm, tn), lambda i, j: (i, j)),
        out_shape=jax.ShapeDtypeStruct((n, PROJ_W), f32),
        scratch_shapes=[pltpu.VMEM((tm, d), bf16)],
        compiler_params=pltpu.CompilerParams(
            dimension_semantics=("parallel", "arbitrary"), vmem_limit_bytes=VMEM_LIMIT),
        name="in_proj",
    )(h2d, g, w)


def _norm_rope(x, gain, cos, sin_signed, gsum, scale):
    ms = _mm_exact_rhs(x * x, gsum)
    xn = x * lax.rsqrt(ms + NORM_EPS) * gain
    lane = lax.broadcasted_iota(jnp.int32, x.shape, 1)
    first_half = (lane % AXIS_DIM) < AXIS_FREQS
    partner = jnp.where(first_half,
                        pltpu.roll(xn, LANES - AXIS_FREQS, axis=1),
                        pltpu.roll(xn, AXIS_FREQS, axis=1))
    out = xn * cos + partner * sin_signed
    return out * scale if scale != 1.0 else out


def _qkv_kernel(q_ref, k_ref, v_ref, cos_ref, sin_ref, qg_ref, kg_ref, qo_ref, ko_ref, vo_ref):
    gsum = _head_sum_matrix(1.0 / HEAD_DIM)
    cos = cos_ref[...]
    sin = sin_ref[...]
    for i in range(0, ATTN_Q_WIDTH, LANES):
        qo_ref[:, i:i + LANES] = _norm_rope(q_ref[:, i:i + LANES], qg_ref[...], cos, sin, gsum,
                                            HEAD_DIM ** -0.5).astype(bf16)
    ko_ref[...] = _norm_rope(k_ref[...], kg_ref[...], cos, sin, gsum, 1.0).astype(bf16)
    vo_ref[...] = v_ref[...].astype(bf16)


def _qkv_prep(proj3, cos, sin, qg, kg):
    b, lp, _ = proj3.shape
    t = _pick_tile(lp, (1040, 832, 320, 64))
    row = lambda w, cb: pl.BlockSpec((None, t, w), lambda bi, ti: (bi, ti, cb))
    tab = pl.BlockSpec((t, LANES), lambda bi, ti: (ti, 0))
    vec = pl.BlockSpec((1, LANES), lambda bi, ti: (0, 0))
    out = lambda w: pl.BlockSpec((None, t, w), lambda bi, ti: (bi, ti, 0))
    return pl.pallas_call(
        _qkv_kernel,
        grid=(b, lp // t),
        in_specs=[row(ATTN_Q_WIDTH, COL_Q // ATTN_Q_WIDTH),
                  row(ATTN_KV_WIDTH, COL_K // ATTN_KV_WIDTH),
                  row(ATTN_KV_WIDTH, COL_V // ATTN_KV_WIDTH),
                  tab, tab, vec, vec],
        out_specs=[out(ATTN_Q_WIDTH), out(ATTN_KV_WIDTH), out(ATTN_KV_WIDTH)],
        out_shape=[jax.ShapeDtypeStruct((b, lp, ATTN_Q_WIDTH), bf16),
                   jax.ShapeDtypeStruct((b, lp, ATTN_KV_WIDTH), bf16),
                   jax.ShapeDtypeStruct((b, lp, ATTN_KV_WIDTH), bf16)],
        compiler_params=pltpu.CompilerParams(dimension_semantics=("parallel", "parallel")),
        name="qkv_prep",
    )(proj3, proj3, proj3, cos, sin, qg, kg)


def _attn_kernel(q_ref, k_ref, v_ref, bias_ref, o_ref):
    g, tq, n = q_ref.shape
    q = q_ref[...].reshape(g * tq, n)
    s = lax.dot_general(q, k_ref[...], (((1,), (1,)), ((), ())), preferred_element_type=f32)
    s = s + bias_ref[...]
    m = jnp.max(s, axis=-1, keepdims=True)
    p = jnp.exp(s - m)
    l = jnp.sum(p, axis=-1, keepdims=True)
    o = jnp.dot(p.astype(bf16), v_ref[...], preferred_element_type=f32) / l
    o_ref[...] = o.reshape(g, tq, n)


def _attention(q5, k4, v4, bias):
    b, kvh, g, lp, n = q5.shape
    lk = k4.shape[2]
    tq = CHUNK
    return pl.pallas_call(
        _attn_kernel,
        grid=(b, kvh, lp // tq),
        in_specs=[pl.BlockSpec((None, None, g, tq, n), lambda bi, hi, qi: (bi, hi, 0, qi, 0)),
                  pl.BlockSpec((None, None, lk, n), lambda bi, hi, qi: (bi, hi, 0, 0)),
                  pl.BlockSpec((None, None, lk, n), lambda bi, hi, qi: (bi, hi, 0, 0)),
                  pl.BlockSpec((1, lk), lambda bi, hi, qi: (0, 0))],
        out_specs=pl.BlockSpec((None, None, g, tq, n), lambda bi, hi, qi: (bi, hi, 0, qi, 0)),
        out_shape=jax.ShapeDtypeStruct((b, kvh, g, lp, n), f32),
        compiler_params=pltpu.CompilerParams(
            dimension_semantics=("parallel", "parallel", "parallel"), vmem_limit_bytes=VMEM_LIMIT),
        name="attention",
    )(q5, k4, v4, bias)


def _shifted_rows(p, prev_row, next_row):
    rows = lax.broadcasted_iota(jnp.int32, p.shape, 0)
    n = p.shape[0]
    prev = jnp.where(rows == 0, prev_row, pltpu.roll(p, 1, axis=0))
    nxt = jnp.where(rows == n - 1, next_row, pltpu.roll(p, n - 1, axis=0))
    return prev, nxt


def _stack_pair(x):
    lane = lax.broadcasted_iota(jnp.int32, x.shape, 1)
    first = lane < HEAD_DIM
    return jnp.concatenate([jnp.where(first, x, 0.0), jnp.where(first, 0.0, x)], axis=0)


def _chunk_terms(direction, p, prev_row, next_row, chunk_idx, prm, gate_ref, bonus_ref):
    (mu_ref, w0_ref, w2_ref, a0_ref, a2_ref, g2_ref, kk_ref, ka_ref, rk_ref) = prm
    mu = mu_ref[...]
    prev, nxt = _shifted_rows(p, prev_row, next_row)
    z = p + mu[0:1] * (prev - p) + mu[1:2] * (nxt - p)

    rows = lax.broadcasted_iota(jnp.int32, (CHUNK, 1), 0) + chunk_idx * CHUNK
    valid = rows >= FRONT_PAD
    z = jnp.where(valid, z, 0.0)

    r = z[:, OFF_R:OFF_R + RWKV_WIDTH]
    k = z[:, OFF_K:OFF_K + RWKV_WIDTH]
    v = z[:, OFF_V:OFF_V + RWKV_WIDTH]
    lo, hi = direction * RWKV_WIDTH, (direction + 1) * RWKV_WIDTH

    x_w = w0_ref[...] + _mm(jnp.tanh(z[:, OFF_WD:OFF_AD]), w2_ref[...])
    logw = -jnp.exp(f32(-0.5)) * jax.nn.sigmoid(x_w[:, lo:hi])
    logw = jnp.where(valid, logw, 0.0)
    a_all = jax.nn.sigmoid(a0_ref[...] + _mm(z[:, OFF_AD:OFF_GD], a2_ref[...]))
    a_dir = a_all[:, lo:hi]

    gsum = _head_sum_matrix(1.0)
    kk = k * kk_ref[...]
    kk = kk / jnp.maximum(jnp.sqrt(_head_sums(kk * kk, gsum)), 1e-12)
    k_a = ka_ref[...]
    k_dir = k * (1.0 + (a_dir - 1.0) * k_a)
    kka = kk * a_dir

    if direction == 0:
        gate_ref[...] = _mm(jax.nn.sigmoid(z[:, OFF_GD:RWKV_IN_PAD]), g2_ref[...])
        a_mean = 0.5 * (a_all[:, :RWKV_WIDTH] + a_all[:, RWKV_WIDTH:])
        k_mean = k * (1.0 + (a_mean - 1.0) * k_a)
        bonus_ref[...] = _head_sums(r * k_mean * rk_ref[...], gsum) * v

    tr = lax.broadcasted_iota(jnp.int32, (CHUNK, CHUNK), 0)
    tc = lax.broadcasted_iota(jnp.int32, (CHUNK, CHUNK), 1)
    tri = (tc <= tr) if direction == 0 else (tc >= tr)
    lc = _mm_exact_lhs(jnp.where(tri, 1.0, 0.0).astype(bf16), logw)
    last = CHUNK - 1 if direction == 0 else 0
    ltot = lc[last:last + 1, :]
    e_in = jnp.exp(lc)
    e_ex = jnp.exp(lc - logw)
    e_neg = jnp.exp(-lc)
    e_end = jnp.exp(ltot - lc)
    r_t = r * e_in
    a_t = -kk * e_ex
    b_t = kka * e_neg
    k_t = k_dir * e_neg
    b_e = kka * e_end
    k_e = k_dir * e_end
    e_tot = jnp.exp(ltot)
    return dict(r=r_t, a=a_t, b=b_t, k=k_t, v=v, be=b_e, ke=k_e, e_tot=e_tot)


def _scan_units(terms, h_ref, y_refs):
    sr = lax.broadcasted_iota(jnp.int32, (PAIR, PAIR), 0)
    sc = lax.broadcasted_iota(jnp.int32, (PAIR, PAIR), 1)
    same = (sr // CHUNK) == (sc // CHUNK)
    tt, ss = sr % CHUNK, sc % CHUNK
    strict = (same & (ss < tt), same & (ss > tt))
    incl = (same & (ss <= tt), same & (ss >= tt))
    eye = sr == sc
    eye_f = jnp.where(eye, 1.0, 0.0)
    levels = []
    m = 2
    while m < CHUNK:
        levels.append(((sr // (2 * m)) == (sc // (2 * m))) & ((sr // m) != (sc // m)))
        m *= 2
    base = (sr // 2) == (sc // 2)

    nt = ((1,), (1,))
    tn = ((0,), (0,))
    cat0 = lambda u, w: jnp.concatenate([u, w], axis=0)
    cat1 = lambda u, w: jnp.concatenate([u, w], axis=1)
    units = [(d, j) for d in range(2) for j in range(N_PAIRS)]
    sl = lambda j: slice(j * PAIR, (j + 1) * PAIR)
    stacked = lambda name, dt: [_stack_pair(terms[d][name][:, sl(j)]).astype(dt) for d, j in units]

    h0 = [h_ref[d, j] for d, j in units]
    a_s, b_s, k_s = stacked("a", bf16), stacked("b", bf16), stacked("k", bf16)
    v_s, be_s, ke_s = stacked("v", bf16), stacked("be", bf16), stacked("ke", bf16)
    r_s = stacked("r", f32)

    scores = [_dot(cat0(a_s[u], r_s[u].astype(bf16)), cat0(b_s[u], k_s[u]), nt)
              for u in range(len(units))]
    a_ab = [jnp.where(strict[d], scores[u][:PAIR, :PAIR], 0.0) for u, (d, _) in enumerate(units)]
    a_ak = [jnp.where(strict[d], scores[u][:PAIR, PAIR:], 0.0).astype(bf16)
            for u, (d, _) in enumerate(units)]
    a_rb = [jnp.where(incl[d], scores[u][PAIR:, :PAIR], 0.0).astype(bf16)
            for u, (d, _) in enumerate(units)]
    a_rk = [jnp.where(incl[d], scores[u][PAIR:, PAIR:], 0.0).astype(bf16)
            for u, (d, _) in enumerate(units)]

    t_inv = [eye_f + jnp.where(base, a, 0.0) for a in a_ab]
    for off in levels:
        tb = [t.astype(bf16) for t in t_inv]
        at = [_dot(jnp.where(off, a, 0.0).astype(bf16), t) for a, t in zip(a_ab, tb)]
        t_inv = [t + _dot(t16, x.astype(bf16)) for t, t16, x in zip(t_inv, tb, at)]

    av = [_dot(a, v) for a, v in zip(a_ak, v_s)]
    w2 = [_dot(t.astype(bf16), cat1(a, x.astype(bf16))).astype(bf16)
          for t, a, x in zip(t_inv, a_s, av)]
    qy = [_dot(a, w) for a, w in zip(a_rb, w2)]
    y0b = [_dot(a, v) for a, v in zip(a_rk, v_s)]
    mn = [_dot(b, w, tn) for b, w in zip(be_s, w2)]
    kv = [_dot(k, v, tn) for k, v in zip(ke_s, v_s)]
    qm = [_dot(cat0(r + x[:, :PAIR], y[:, :PAIR]).astype(bf16), h.astype(bf16))
          for r, x, y, h in zip(r_s, qy, mn, h0)]
    for u, (d, j) in enumerate(units):
        y_st = qm[u][:PAIR] + qy[u][:, PAIR:] + y0b[u]
        y_refs[d][:, sl(j)] = y_st[:CHUNK] + y_st[CHUNK:]
        decay_col = jnp.sum(jnp.where(eye, terms[d]["e_tot"][:, sl(j)], 0.0), axis=1, keepdims=True)
        h_ref[d, j] = decay_col * h0[u] + qm[u][PAIR:] + mn[u][:, PAIR:] + kv[u]


def _rwkv_kernel(pf_ref, pf_prev_ref, pf_next_ref, pb_ref, pb_prev_ref, pb_next_ref,
                 mu_ref, w0_ref, w2_ref, a0_ref, a2_ref, g2_ref, kk_ref, ka_ref, rk_ref,
                 yf_ref, yb_ref, gate_ref, bonus_ref, h_ref, *, n_chunks):
    c = pl.program_id(1)

    @pl.when(c == 0)
    def _():
        h_ref[...] = jnp.zeros_like(h_ref)

    prm = (mu_ref, w0_ref, w2_ref, a0_ref, a2_ref, g2_ref, kk_ref, ka_ref, rk_ref)
    terms = []
    for direction, (p_ref, prev_ref, next_ref) in enumerate(
            ((pf_ref, pf_prev_ref, pf_next_ref), (pb_ref, pb_prev_ref, pb_next_ref))):
        ci = c if direction == 0 else n_chunks - 1 - c
        prev_row = jnp.where(ci == 0, 0.0, prev_ref[HALO - 1:HALO, :])
        next_row = jnp.where(ci == n_chunks - 1, 0.0, next_ref[0:1, :])
        terms.append(_chunk_terms(direction, p_ref[...], prev_row, next_row, ci, prm,
                                  gate_ref, bonus_ref))
    _scan_units(terms, h_ref, (yf_ref, yb_ref))


def _rwkv(proj3, mu, w0, w2, a0, a2, g2, k_k, k_a, r_k):
    b, lp, _ = proj3.shape
    nc = lp // CHUNK
    hb = CHUNK // HALO
    cb = COL_RWKV // RWKV_IN_PAD
    fwd = lambda bi, c: (bi, c, cb)
    bwd = lambda bi, c: (bi, nc - 1 - c, cb)
    fwd_prev = lambda bi, c: (bi, jnp.maximum(c * hb - 1, 0), cb)
    fwd_next = lambda bi, c: (bi, jnp.minimum((c + 1) * hb, nc * hb - 1), cb)
    bwd_prev = lambda bi, c: (bi, jnp.maximum((nc - 1 - c) * hb - 1, 0), cb)
    bwd_next = lambda bi, c: (bi, jnp.minimum((nc - c) * hb, nc * hb - 1), cb)
    main = lambda im: pl.BlockSpec((None, CHUNK, RWKV_IN_PAD), im)
    halo = lambda im: pl.BlockSpec((None, HALO, RWKV_IN_PAD), im)
    full = lambda a: pl.BlockSpec(a.shape, lambda bi, c: (0,) * a.ndim)
    out = lambda im: pl.BlockSpec((None, CHUNK, RWKV_WIDTH), im)
    o_fwd = lambda bi, c: (bi, c, 0)
    o_bwd = lambda bi, c: (bi, nc - 1 - c, 0)
    shp = jax.ShapeDtypeStruct((b, lp, RWKV_WIDTH), f32)
    params = (mu, w0, w2, a0, a2, g2, k_k, k_a, r_k)
    return pl.pallas_call(
        functools.partial(_rwkv_kernel, n_chunks=nc),
        grid=(b, nc),
        in_specs=[main(fwd), halo(fwd_prev), halo(fwd_next),
                  main(bwd), halo(bwd_prev), halo(bwd_next)] + [full(a) for a in params],
        out_specs=[out(o_fwd), out(o_bwd), out(o_fwd), out(o_fwd)],
        out_shape=[shp, shp, shp, shp],
        scratch_shapes=[pltpu.VMEM((2, N_PAIRS, PAIR, PAIR), f32)],
        compiler_params=pltpu.CompilerParams(
            dimension_semantics=("parallel", "arbitrary"), vmem_limit_bytes=VMEM_LIMIT),
        name="rwkv",
    )(proj3, proj3, proj3, proj3, proj3, proj3, *params)


def _post_kernel(yf_ref, yb_ref, gate_ref, bonus_ref, att_ref, ga_ref, gb_ref, h_ref,
                 lng_ref, lnb_ref, wa_ref, wb_ref, wo_ref, o_ref):
    gmean = _head_sum_matrix(1.0 / HEAD_DIM)
    y = yf_ref[...] + yb_ref[...]
    mu = _head_sums(y, gmean)
    d = y - mu
    var = _head_sums(d * d, gmean)
    yn = d * lax.rsqrt(var + LNX_EPS) * lng_ref[...] + lnb_ref[...]
    out_a = ((yn + bonus_ref[...]) * gate_ref[...]).astype(bf16)
    ya = jnp.dot(out_a, wa_ref[...], preferred_element_type=f32)
    yb = jnp.dot(att_ref[...].astype(bf16), wb_ref[...], preferred_element_type=f32)
    merged = jax.nn.sigmoid(ga_ref[...]) * ya + jax.nn.sigmoid(gb_ref[...]) * yb
    o_ref[...] = h_ref[...] + jnp.dot(merged.astype(bf16), wo_ref[...], preferred_element_type=f32)


def _post(yf, yb, gate, bonus, att, proj, h2d, lng, lnb, wa, wb, wo):
    n, d = h2d.shape
    tm = _pick_tile(n, (640, 320, 128, 64))
    row = lambda w, cb=0: pl.BlockSpec((tm, w), lambda i: (i, cb))
    full = lambda a: pl.BlockSpec(a.shape, lambda i: (0,) * a.ndim)
    return pl.pallas_call(
        _post_kernel,
        grid=(n // tm,),
        in_specs=[row(RWKV_WIDTH), row(RWKV_WIDTH), row(RWKV_WIDTH), row(RWKV_WIDTH),
                  row(ATTN_Q_WIDTH), row(d, COL_GATE_A // d), row(d, COL_GATE_B // d), row(d),
                  full(lng), full(lnb), full(wa), full(wb), full(wo)],
        out_specs=row(d),
        out_shape=jax.ShapeDtypeStruct((n, d), f32),
        compiler_params=pltpu.CompilerParams(
            dimension_semantics=("parallel",), vmem_limit_bytes=VMEM_LIMIT),
        name="post",
    )(yf, yb, gate, bonus, att, proj, proj, h2d, lng, lnb, wa, wb, wo)


def _ffn_kernel(h_ref, g_ref, w1_ref, w2_ref, gf_ref, o_ref, *, ff_chunk):
    h = h_ref[...]
    ms = jnp.mean(h * h, axis=-1, keepdims=True)
    xn = (h * lax.rsqrt(ms + NORM_EPS) * g_ref[...]).astype(bf16)
    acc = h
    for c in range(0, w1_ref.shape[1], ff_chunk):
        f = jnp.dot(xn, w1_ref[:, c:c + ff_chunk], preferred_element_type=f32)
        f = jnp.square(jnp.maximum(f, 0.0)).astype(bf16)
        acc = acc + jnp.dot(f, w2_ref[c:c + ff_chunk, :], preferred_element_type=f32)
    ms2 = jnp.mean(acc * acc, axis=-1, keepdims=True)
    o_ref[...] = acc * lax.rsqrt(ms2 + NORM_EPS) * gf_ref[...]


def _ffn(h2d, g, w1, w2, gf):
    n, d = h2d.shape
    tm = _pick_tile(n, (640, 320, 128, 64))
    row = pl.BlockSpec((tm, d), lambda i: (i, 0))
    full = lambda a: pl.BlockSpec(a.shape, lambda i: (0,) * a.ndim, pipeline_mode=pl.Buffered(1))
    return pl.pallas_call(
        functools.partial(_ffn_kernel, ff_chunk=512),
        grid=(n // tm,),
        in_specs=[row, full(g), full(w1), full(w2), full(gf)],
        out_specs=row,
        out_shape=jax.ShapeDtypeStruct((n, d), f32),
        compiler_params=pltpu.CompilerParams(
            dimension_semantics=("parallel",), vmem_limit_bytes=VMEM_LIMIT),
        name="ffn",
    )(h2d, g, w1, w2, gf)


def _rope_tables(n_tok, lp):
    rows = n_tok // GRID_W
    inv_freq = ROPE_THETA ** (-jnp.arange(AXIS_FREQS, dtype=f32) * 2.0 / AXIS_DIM)
    row_ang = jnp.arange(rows, dtype=f32)[:, None] * inv_freq
    col_ang = jnp.arange(GRID_W, dtype=f32)[:, None] * inv_freq
    grid = jnp.stack([jnp.broadcast_to(row_ang[:, None, :], (rows, GRID_W, AXIS_FREQS)),
                      jnp.broadcast_to(col_ang[None, :, :], (rows, GRID_W, AXIS_FREQS))], axis=2)
    grid = grid.reshape(rows * GRID_W, 2, AXIS_FREQS)
    ang = jnp.concatenate([jnp.zeros((lp - n_tok, 2, AXIS_FREQS), f32), grid], axis=0)
    cos, sin = jnp.cos(ang), jnp.sin(ang)
    cos64 = jnp.stack([cos, cos], axis=2).reshape(lp, HEAD_DIM)
    sin64 = jnp.stack([-sin, sin], axis=2).reshape(lp, HEAD_DIM)
    return jnp.tile(cos64, (1, 2)), jnp.tile(sin64, (1, 2))


def kernel(x, meta_tokens, mix_norm_g, w_in, rwkv_shift, decay_w0, decay_w2, icl_a0, icl_a2, gate_w2, k_k, k_a, r_k, lnx_g, lnx_b, q_norm_g, k_norm_g, w_branch_rwkv, w_branch_attn, w_out, ffn_norm_g, w_ff1, w_ff2, final_norm_g):
    b, n_tok, d = x.shape
    lp = CHUNK + n_tok
    n = b * lp

    meta = jnp.broadcast_to(meta_tokens[None].astype(x.dtype), (b, N_META, d))
    h = jnp.concatenate([jnp.zeros((b, FRONT_PAD, d), x.dtype), meta, x], axis=1)
    h2d = h.reshape(n, d)

    w = w_in[0]
    s0 = RWKV_IN
    s1 = s0 + ATTN_Q_WIDTH
    s2 = s1 + ATTN_KV_WIDTH
    s3 = s2 + ATTN_KV_WIDTH
    s4 = s3 + d
    w_perm = jnp.concatenate(
        [w[:, s3:s4], w[:, s4:], w[:, :s0], jnp.zeros((d, RWKV_IN_PAD - RWKV_IN), w.dtype),
         w[:, s0:s1], w[:, s1:s2], w[:, s2:s3]], axis=1).astype(bf16)
    mu = jnp.pad(rwkv_shift[0], ((0, 0), (0, RWKV_IN_PAD - RWKV_IN)))
    zl = jnp.zeros((DECAY_LORA, RWKV_WIDTH), f32)
    w2cat = jnp.concatenate([jnp.concatenate([decay_w2[0, 0], zl], axis=1),
                             jnp.concatenate([zl, decay_w2[0, 1]], axis=1)], axis=0)
    a2cat = jnp.concatenate([jnp.concatenate([icl_a2[0, 0], zl], axis=1),
                             jnp.concatenate([zl, icl_a2[0, 1]], axis=1)], axis=0)
    w0cat = decay_w0[0].reshape(1, 2 * RWKV_WIDTH)
    a0cat = icl_a0[0].reshape(1, 2 * RWKV_WIDTH)
    g2pad = jnp.pad(gate_w2[0], ((0, GD_PAD - GATE_LORA), (0, 0)))
    row = lambda a: a.reshape(1, -1)

    proj = _in_proj(h2d, row(mix_norm_g[0]), w_perm)
    proj3 = proj.reshape(b, lp, PROJ_W)

    cos, sin = _rope_tables(n_tok, lp)
    qg = jnp.tile(row(q_norm_g[0]), (1, 2))
    kg = jnp.tile(row(k_norm_g[0]), (1, 2))
    q, k, v = _qkv_prep(proj3, cos, sin, qg, kg)
    lk = -(-lp // LANES) * LANES
    q5 = q.reshape(b, lp, ATTN_KV_HEADS, ATTN_GROUP, HEAD_DIM).transpose(0, 2, 3, 1, 4)
    kv_layout = lambda t: jnp.pad(
        t.reshape(b, lp, ATTN_KV_HEADS, HEAD_DIM).transpose(0, 2, 1, 3),
        ((0, 0), (0, 0), (0, lk - lp), (0, 0)))
    pos = jnp.arange(lk)
    bias = jnp.where((pos >= FRONT_PAD) & (pos < lp), 0.0, MASK_BIAS).astype(f32)[None]
    att = _attention(q5, kv_layout(k), kv_layout(v), bias)
    att = att.transpose(0, 3, 1, 2, 4).reshape(n, ATTN_Q_WIDTH)

    yf, yb, gate, bonus = _rwkv(proj3, mu, w0cat, w2cat, a0cat, a2cat, g2pad,
                                row(k_k[0]), row(k_a[0]), row(r_k[0]))
    flat = lambda t: t.reshape(n, RWKV_WIDTH)

    h1 = _post(flat(yf), flat(yb), flat(gate), flat(bonus), att, proj, h2d,
               row(lnx_g[0]), row(lnx_b[0]), w_branch_rwkv[0].astype(bf16),
               w_branch_attn[0].astype(bf16), w_out[0].astype(bf16))
    out = _ffn(h1, row(ffn_norm_g[0]), w_ff1[0].astype(bf16), w_ff2[0].astype(bf16),
               row(final_norm_g))
    return out.reshape(b, lp, d)[:, CHUNK:]
```

```python
import functools

import jax
import jax.numpy as jnp
from jax import lax
from jax.experimental import pallas as pl
from jax.experimental.pallas import tpu as pltpu

f32 = jnp.float32
bf16 = jnp.bfloat16

N_META = 16
GRID_W = 64
HEAD_DIM = 64
RWKV_HEADS = 8
RWKV_WIDTH = RWKV_HEADS * HEAD_DIM
DECAY_LORA = 64
ICL_LORA = 64
GATE_LORA = 160
LNX_EPS = 64e-5
ATTN_Q_HEADS = 8
ATTN_KV_HEADS = 2
ATTN_GROUP = ATTN_Q_HEADS // ATTN_KV_HEADS
ATTN_Q_WIDTH = ATTN_Q_HEADS * HEAD_DIM
ATTN_KV_WIDTH = ATTN_KV_HEADS * HEAD_DIM
ROPE_THETA = 10000.0
AXIS_DIM = HEAD_DIM // 2
AXIS_FREQS = AXIS_DIM // 2
NORM_EPS = 1e-6

LANES = 128
CHUNK = 64
HALO = 8
PAIR = 2 * HEAD_DIM
N_PAIRS = RWKV_WIDTH // PAIR
ROW_TILE = 512
FRONT = ROW_TILE
META_ROW0 = FRONT - N_META
CHUNK0 = META_ROW0 // CHUNK
Q_TILE = 64
KEY_ROW0 = FRONT - LANES
LOG2E = 1.4426950408889634
MASK_BIAS = -1e30
VMEM_LIMIT = 56 * 1024 * 1024

RWKV_IN = 3 * RWKV_WIDTH + 2 * DECAY_LORA + 2 * ICL_LORA + GATE_LORA
RWKV_IN_PAD = 2048
COL_GATE_A = 0
COL_GATE_B = 1024
COL_RWKV = 2048
COL_Q = 4096
COL_K = 4608
COL_V = 4736
PROJ_W = 4864
OFF_R, OFF_K, OFF_V = 0, RWKV_WIDTH, 2 * RWKV_WIDTH
OFF_WD = 3 * RWKV_WIDTH
OFF_AD = OFF_WD + 2 * DECAY_LORA
OFF_GD = OFF_AD + 2 * ICL_LORA
GD_PAD = RWKV_IN_PAD - OFF_GD


def _pick_tile(n, candidates):
    for c in candidates:
        if n % c == 0:
            return c
    raise ValueError(f"no tile for {n} in {candidates}")


def _split3(x):
    h = x.astype(bf16)
    r = x - h.astype(f32)
    m = r.astype(bf16)
    l = (r - m.astype(f32)).astype(bf16)
    return h, m, l


def _dot(a, b, dims=None):
    if dims is None:
        return jnp.dot(a, b, preferred_element_type=f32)
    return lax.dot_general(a, b, (dims, ((), ())), preferred_element_type=f32)


def _mm(a, b):
    return _dot(a.astype(bf16), b.astype(bf16))


def _mm_exact_lhs(a_bf16, b):
    bh, bm, bl = _split3(b)
    return _dot(a_bf16, bh) + _dot(a_bf16, bm) + _dot(a_bf16, bl)


def _mm_exact_rhs(a, b_bf16):
    ah, am, al = _split3(a)
    return _dot(ah, b_bf16) + _dot(am, b_bf16) + _dot(al, b_bf16)


def _head_sum_matrix(scale):
    r = lax.broadcasted_iota(jnp.int32, (LANES, LANES), 0) // HEAD_DIM
    c = lax.broadcasted_iota(jnp.int32, (LANES, LANES), 1) // HEAD_DIM
    return jnp.where(r == c, scale, 0.0).astype(bf16)


def _head_sums(x, g):
    w = x.shape[-1]
    parts = [_mm_exact_rhs(x[:, i:i + LANES], g) for i in range(0, w, LANES)]
    return parts[0] if len(parts) == 1 else jnp.concatenate(parts, axis=-1)


def _resident(a, n_grid):
    zeros = (0,) * a.ndim
    return pl.BlockSpec(a.shape, lambda *_: zeros, pipeline_mode=pl.Buffered(1))


def _in_proj_kernel(mf_ref, x_ref, g_ref, w_ref, o_ref, u_ref):
    def norm_to(src_ref):
        x = src_ref[...]
        ms = jnp.mean(x * x, axis=-1, keepdims=True)
        u_ref[...] = (x * lax.rsqrt(ms + NORM_EPS) * g_ref[...]).astype(bf16)

    i = pl.program_id(1)
    pl.when(i == 0)(lambda: norm_to(mf_ref))
    pl.when(i > 0)(lambda: norm_to(x_ref))
    o_ref[...] = jnp.dot(u_ref[...], w_ref[...], preferred_element_type=f32)


def _in_proj(meta_frame, x, g, w):
    b, n_tok, d = x.shape
    nx = n_tok // ROW_TILE
    return pl.pallas_call(
        _in_proj_kernel,
        grid=(b, nx + 1),
        in_specs=[_resident(meta_frame, 2),
                  pl.BlockSpec((None, ROW_TILE, d), lambda bi, i: (bi, jnp.maximum(i - 1, 0), 0)),
                  _resident(g, 2), _resident(w, 2)],
        out_specs=pl.BlockSpec((None, ROW_TILE, PROJ_W), lambda bi, i: (bi, i, 0)),
        out_shape=jax.ShapeDtypeStruct((b, FRONT + n_tok, PROJ_W), f32),
        scratch_shapes=[pltpu.VMEM((ROW_TILE, d), bf16)],
        compiler_params=pltpu.CompilerParams(
            dimension_semantics=("parallel", "arbitrary"), vmem_limit_bytes=VMEM_LIMIT),
        name="in_proj",
    )(meta_frame, x, g, w)


def _norm_rope(x, gain, cos, sin_signed, gsum, scale):
    ms = _mm_exact_rhs(x * x, gsum)
    xn = x * lax.rsqrt(ms + NORM_EPS) * gain
    lane = lax.broadcasted_iota(jnp.int32, x.shape, 1)
    first_half = (lane % AXIS_DIM) < AXIS_FREQS
    partner = jnp.where(first_half,
                        pltpu.roll(xn, LANES - AXIS_FREQS, axis=1),
                        pltpu.roll(xn, AXIS_FREQS, axis=1))
    out = xn * cos + partner * sin_signed
    return out * scale if scale != 1.0 else out


def _qkv_kernel(q_ref, k_ref, v_ref, cos_ref, sin_ref, qg_ref, kg_ref, qo_ref, ko_ref, vo_ref):
    gsum = _head_sum_matrix(1.0 / HEAD_DIM)
    cos = cos_ref[...]
    sin = sin_ref[...]
    for i in range(0, ATTN_Q_WIDTH, LANES):
        qo_ref[:, i:i + LANES] = _norm_rope(q_ref[:, i:i + LANES], qg_ref[...], cos, sin, gsum,
                                            HEAD_DIM ** -0.5 * LOG2E).astype(bf16)
    kn = _norm_rope(k_ref[...], kg_ref[...], cos, sin, gsum, 1.0)
    vf = v_ref[...]
    t = kn.shape[0]
    lane = lax.broadcasted_iota(jnp.int32, kn.shape, 1)
    key_row = lax.broadcasted_iota(jnp.int32, kn.shape, 0) + pl.program_id(1) * t
    extra = lane == HEAD_DIM
    k_aug = jnp.where(extra & (key_row < META_ROW0 - KEY_ROW0), MASK_BIAS, 0.0)
    v_aug = jnp.where(extra, 1.0, 0.0)
    for h in range(ATTN_KV_HEADS):
        kh = kn if h == 0 else pltpu.roll(kn, HEAD_DIM, axis=1)
        vh = vf if h == 0 else pltpu.roll(vf, HEAD_DIM, axis=1)
        ko_ref[h] = jnp.where(lane < HEAD_DIM, kh, k_aug).astype(bf16)
        vo_ref[h] = jnp.where(lane < HEAD_DIM, vh, v_aug).astype(bf16)


def _qkv_prep(proj3, cos, sin, qg, kg):
    b, lp, _ = proj3.shape
    lk = lp - KEY_ROW0
    t = _pick_tile(lk, (KEY_ROW0, LANES))
    off = KEY_ROW0 // t
    row = lambda w, cb: pl.BlockSpec((None, t, w), lambda bi, ti: (bi, ti + off, cb))
    tab = pl.BlockSpec((t, LANES), lambda bi, ti: (ti, 0))
    vec = pl.BlockSpec((1, LANES), lambda bi, ti: (0, 0))
    kv_out = pl.BlockSpec((None, ATTN_KV_HEADS, t, LANES), lambda bi, ti: (bi, 0, ti, 0))
    kv_shape = jax.ShapeDtypeStruct((b, ATTN_KV_HEADS, lk, LANES), bf16)
    return pl.pallas_call(
        _qkv_kernel,
        grid=(b, lk // t),
        in_specs=[row(ATTN_Q_WIDTH, COL_Q // ATTN_Q_WIDTH),
                  row(ATTN_KV_WIDTH, COL_K // ATTN_KV_WIDTH),
                  row(ATTN_KV_WIDTH, COL_V // ATTN_KV_WIDTH),
                  tab, tab, vec, vec],
        out_specs=[pl.BlockSpec((None, t, ATTN_Q_WIDTH), lambda bi, ti: (bi, ti, 0)),
                   kv_out, kv_out],
        out_shape=[jax.ShapeDtypeStruct((b, lk, ATTN_Q_WIDTH), bf16), kv_shape, kv_shape],
        compiler_params=pltpu.CompilerParams(dimension_semantics=("parallel", "parallel")),
        name="qkv_prep",
    )(proj3, proj3, proj3, cos, sin, qg, kg)


def _attn_kernel(q_ref, k_ref, v_ref, o_ref):
    tq = q_ref.shape[0]
    qf = q_ref[...].astype(f32)
    lane = lax.broadcasted_iota(jnp.int32, (tq, LANES), 1)
    one_hot = jnp.where(lane == HEAD_DIM, 1.0, 0.0)
    rows = []
    for g in range(ATTN_GROUP):
        blk = qf[:, (g // 2) * LANES:(g // 2 + 1) * LANES]
        if g % 2:
            blk = pltpu.roll(blk, HEAD_DIM, axis=1)
        rows.append(jnp.where(lane < HEAD_DIM, blk, one_hot))
    q = jnp.concatenate(rows, axis=0).astype(bf16)
    s = lax.dot_general(q, k_ref[...], (((1,), (1,)), ((), ())), preferred_element_type=f32)
    m = jnp.max(s, axis=-1, keepdims=True)
    p = jnp.exp2(s - m).astype(bf16)
    ov = jnp.dot(p, v_ref[...], preferred_element_type=f32)
    on = ov / ov[:, HEAD_DIM:HEAD_DIM + 1]
    for g in range(0, ATTN_GROUP, 2):
        even = on[g * tq:(g + 1) * tq]
        odd = pltpu.roll(on[(g + 1) * tq:(g + 2) * tq], HEAD_DIM, axis=1)
        o_ref[:, (g // 2) * LANES:(g // 2 + 1) * LANES] = jnp.where(lane < HEAD_DIM, even, odd)


def _attention(q, k4, v4, n_tok):
    b, kvh, lk, n = k4.shape
    gw = ATTN_GROUP * HEAD_DIM
    q_off = (FRONT - KEY_ROW0) // Q_TILE
    return pl.pallas_call(
        _attn_kernel,
        grid=(b, kvh, n_tok // Q_TILE),
        in_specs=[pl.BlockSpec((None, Q_TILE, gw), lambda bi, hi, qi: (bi, qi + q_off, hi)),
                  pl.BlockSpec((None, None, lk, n), lambda bi, hi, qi: (bi, hi, 0, 0)),
                  pl.BlockSpec((None, None, lk, n), lambda bi, hi, qi: (bi, hi, 0, 0))],
        out_specs=pl.BlockSpec((None, Q_TILE, gw), lambda bi, hi, qi: (bi, qi, hi)),
        out_shape=jax.ShapeDtypeStruct((b, n_tok, ATTN_Q_WIDTH), f32),
        compiler_params=pltpu.CompilerParams(
            dimension_semantics=("parallel", "parallel", "parallel"), vmem_limit_bytes=VMEM_LIMIT),
        name="attention",
    )(q, k4, v4)


def _shifted_rows(p, prev_row, next_row):
    rows = lax.broadcasted_iota(jnp.int32, p.shape, 0)
    n = p.shape[0]
    prev = jnp.where(rows == 0, prev_row, pltpu.roll(p, 1, axis=0))
    nxt = jnp.where(rows == n - 1, next_row, pltpu.roll(p, n - 1, axis=0))
    return prev, nxt


def _stack_pair(x):
    lane = lax.broadcasted_iota(jnp.int32, x.shape, 1)
    first = lane < HEAD_DIM
    return jnp.concatenate([jnp.where(first, x, 0.0), jnp.where(first, 0.0, x)], axis=0)


def _chunk_terms(direction, p, prev_row, next_row, chunk_idx, prm):
    (mu_ref, w0_ref, w2_ref, a0_ref, a2_ref, g2_ref, kk_ref, ka_ref, rk_ref) = prm
    mu = mu_ref[...]
    prev, nxt = _shifted_rows(p, prev_row, next_row)
    z = p + mu[0:1] * (prev - p) + mu[1:2] * (nxt - p)

    rows = lax.broadcasted_iota(jnp.int32, (CHUNK, 1), 0) + chunk_idx * CHUNK
    valid = rows >= META_ROW0
    z = jnp.where(valid, z, 0.0)

    r = z[:, OFF_R:OFF_R + RWKV_WIDTH]
    k = z[:, OFF_K:OFF_K + RWKV_WIDTH]
    v = z[:, OFF_V:OFF_V + RWKV_WIDTH]
    lo, hi = direction * RWKV_WIDTH, (direction + 1) * RWKV_WIDTH

    x_w = w0_ref[...] + _mm(jnp.tanh(z[:, OFF_WD:OFF_AD]), w2_ref[...])
    logw = -jnp.exp(f32(-0.5)) * jax.nn.sigmoid(x_w[:, lo:hi])
    logw = jnp.where(valid, logw, 0.0)
    a_all = jax.nn.sigmoid(a0_ref[...] + _mm(z[:, OFF_AD:OFF_GD], a2_ref[...]))
    a_dir = a_all[:, lo:hi]

    gsum = _head_sum_matrix(1.0)
    kk = k * kk_ref[...]
    kk = kk / jnp.maximum(jnp.sqrt(_head_sums(kk * kk, gsum)), 1e-12)
    k_a = ka_ref[...]
    k_dir = k * (1.0 + (a_dir - 1.0) * k_a)
    kka = kk * a_dir

    extra = {}
    if direction == 0:
        extra["gate"] = _mm(jax.nn.sigmoid(z[:, OFF_GD:RWKV_IN_PAD]), g2_ref[...])
        a_mean = 0.5 * (a_all[:, :RWKV_WIDTH] + a_all[:, RWKV_WIDTH:])
        k_mean = k * (1.0 + (a_mean - 1.0) * k_a)
        extra["bonus"] = _head_sums(r * k_mean * rk_ref[...], gsum) * v

    tr = lax.broadcasted_iota(jnp.int32, (CHUNK, CHUNK), 0)
    tc = lax.broadcasted_iota(jnp.int32, (CHUNK, CHUNK), 1)
    tri = (tc <= tr) if direction == 0 else (tc >= tr)
    lc = _mm_exact_lhs(jnp.where(tri, 1.0, 0.0).astype(bf16), logw)
    last = CHUNK - 1 if direction == 0 else 0
    ltot = lc[last:last + 1, :]
    e_in = jnp.exp(lc)
    e_ex = jnp.exp(lc - logw)
    e_neg = jnp.exp(-lc)
    e_end = jnp.exp(ltot - lc)
    return dict(r=r * e_in, a=-kk * e_ex, b=kka * e_neg, k=k_dir * e_neg, v=v,
                be=kka * e_end, ke=k_dir * e_end, e_tot=jnp.exp(ltot), **extra)


def _scan_units(terms, h_ref, y_refs, live):
    sr = lax.broadcasted_iota(jnp.int32, (PAIR, PAIR), 0)
    sc = lax.broadcasted_iota(jnp.int32, (PAIR, PAIR), 1)
    same = (sr // CHUNK) == (sc // CHUNK)
    tt, ss = sr % CHUNK, sc % CHUNK
    strict = (same & (ss < tt), same & (ss > tt))
    incl = (same & (ss <= tt), same & (ss >= tt))
    eye = sr == sc
    eye_f = jnp.where(eye, 1.0, 0.0)
    levels = []
    m = 2
    while m < CHUNK:
        levels.append(((sr // (2 * m)) == (sc // (2 * m))) & ((sr // m) != (sc // m)))
        m *= 2
    base = (sr // 2) == (sc // 2)

    nt = ((1,), (1,))
    tn = ((0,), (0,))
    cat0 = lambda u, w: jnp.concatenate([u, w], axis=0)
    cat1 = lambda u, w: jnp.concatenate([u, w], axis=1)
    units = [(d, j) for d in range(2) for j in range(N_PAIRS)]
    sl = lambda j: slice(j * PAIR, (j + 1) * PAIR)
    stacked = lambda name, dt: [_stack_pair(terms[d][name][:, sl(j)]).astype(dt) for d, j in units]

    h0 = [h_ref[d, j] for d, j in units]
    a_s, b_s, k_s = stacked("a", bf16), stacked("b", bf16), stacked("k", bf16)
    v_s, be_s, ke_s = stacked("v", bf16), stacked("be", bf16), stacked("ke", bf16)
    r_s = stacked("r", f32)

    scores = [_dot(cat0(a_s[u], r_s[u].astype(bf16)), cat0(b_s[u], k_s[u]), nt)
              for u in range(len(units))]
    a_ab = [jnp.where(strict[d], scores[u][:PAIR, :PAIR], 0.0) for u, (d, _) in enumerate(units)]
    a_ak = [jnp.where(strict[d], scores[u][:PAIR, PAIR:], 0.0).astype(bf16)
            for u, (d, _) in enumerate(units)]
    a_rb = [jnp.where(incl[d], scores[u][PAIR:, :PAIR], 0.0).astype(bf16)
            for u, (d, _) in enumerate(units)]
    a_rk = [jnp.where(incl[d], scores[u][PAIR:, PAIR:], 0.0).astype(bf16)
            for u, (d, _) in enumerate(units)]

    t_inv = [eye_f + jnp.where(base, a, 0.0) for a in a_ab]
    for off in levels:
        tb = [t.astype(bf16) for t in t_inv]
        at = [_dot(jnp.where(off, a, 0.0).astype(bf16), t) for a, t in zip(a_ab, tb)]
        t_inv = [t + _dot(t16, x.astype(bf16)) for t, t16, x in zip(t_inv, tb, at)]

    av = [_dot(a, v) for a, v in zip(a_ak, v_s)]
    w2 = [_dot(t.astype(bf16), cat1(a, x.astype(bf16))).astype(bf16)
          for t, a, x in zip(t_inv, a_s, av)]
    qy = [_dot(a, w) for a, w in zip(a_rb, w2)]
    y0b = [_dot(a, v) for a, v in zip(a_rk, v_s)]
    mn = [_dot(b, w, tn) for b, w in zip(be_s, w2)]
    kv = [_dot(k, v, tn) for k, v in zip(ke_s, v_s)]
    qm = [_dot(cat0(r + x[:, :PAIR], y[:, :PAIR]).astype(bf16), h.astype(bf16))
          for r, x, y, h in zip(r_s, qy, mn, h0)]
    y_pair = []
    for u, (d, j) in enumerate(units):
        y_st = qm[u][:PAIR] + qy[u][:, PAIR:] + y0b[u]
        y_pair.append(y_st[:CHUNK] + y_st[CHUNK:])
        decay_col = jnp.sum(jnp.where(eye, terms[d]["e_tot"][:, sl(j)], 0.0), axis=1, keepdims=True)
        h_ref[d, j] = decay_col * h0[u] + qm[u][PAIR:] + mn[u][:, PAIR:] + kv[u]

    for d in range(2):
        @pl.when(live[d])
        def _(d=d):
            for u, (du, j) in enumerate(units):
                if du == d:
                    y_refs[d][:, sl(j)] = y_pair[u]


def _rwkv_kernel(pf_ref, pf_prev_ref, pf_next_ref, pb_ref, pb_prev_ref, pb_next_ref,
                 mu_ref, w0_ref, w2_ref, a0_ref, a2_ref, g2_ref, kk_ref, ka_ref, rk_ref,
                 yf_ref, yb_ref, gate_ref, bonus_ref, h_ref, *, n_chunks):
    c = pl.program_id(1)

    @pl.when(c == 0)
    def _():
        h_ref[...] = jnp.zeros_like(h_ref)

    prm = (mu_ref, w0_ref, w2_ref, a0_ref, a2_ref, g2_ref, kk_ref, ka_ref, rk_ref)
    terms, live = [], []
    for direction, (p_ref, prev_ref, next_ref) in enumerate(
            ((pf_ref, pf_prev_ref, pf_next_ref), (pb_ref, pb_prev_ref, pb_next_ref))):
        ci = c if direction == 0 else n_chunks - 1 - c
        prev_row = jnp.where(ci == 0, 0.0, prev_ref[HALO - 1:HALO, :])
        next_row = jnp.where(ci == n_chunks - 1, 0.0, next_ref[0:1, :])
        terms.append(_chunk_terms(direction, p_ref[...], prev_row, next_row, ci + CHUNK0, prm))
        live.append(ci > 0)

    @pl.when(live[0])
    def _():
        gate_ref[...] = terms[0]["gate"]
        bonus_ref[...] = terms[0]["bonus"]

    _scan_units(terms, h_ref, (yf_ref, yb_ref), live)


def _rwkv(proj3, n_tok, mu, w0, w2, a0, a2, g2, k_k, k_a, r_k):
    b, lp, _ = proj3.shape
    nc = lp // CHUNK - CHUNK0
    hb = CHUNK // HALO
    cb = COL_RWKV // RWKV_IN_PAD
    h_last = (lp // CHUNK) * hb - 1
    chunk_of = (lambda c: c + CHUNK0, lambda c: nc - 1 - c + CHUNK0)
    main = lambda d: pl.BlockSpec((None, CHUNK, RWKV_IN_PAD),
                                  lambda bi, c: (bi, chunk_of[d](c), cb))
    prev = lambda d: pl.BlockSpec((None, HALO, RWKV_IN_PAD),
                                  lambda bi, c: (bi, chunk_of[d](c) * hb - 1, cb))
    nxt = lambda d: pl.BlockSpec(
        (None, HALO, RWKV_IN_PAD),
        lambda bi, c: (bi, jnp.minimum((chunk_of[d](c) + 1) * hb, h_last), cb))
    out = lambda d: pl.BlockSpec(
        (None, CHUNK, RWKV_WIDTH),
        lambda bi, c: (bi, jnp.maximum(chunk_of[d](c) - CHUNK0 - 1, 0), 0))
    shp = jax.ShapeDtypeStruct((b, n_tok, RWKV_WIDTH), f32)
    params = (mu, w0, w2, a0, a2, g2, k_k, k_a, r_k)
    return pl.pallas_call(
        functools.partial(_rwkv_kernel, n_chunks=nc),
        grid=(b, nc),
        in_specs=[main(0), prev(0), nxt(0), main(1), prev(1), nxt(1)]
                 + [_resident(a, 2) for a in params],
        out_specs=[out(0), out(1), out(0), out(0)],
        out_shape=[shp, shp, shp, shp],
        scratch_shapes=[pltpu.VMEM((2, N_PAIRS, PAIR, PAIR), f32)],
        compiler_params=pltpu.CompilerParams(
            dimension_semantics=("parallel", "arbitrary"), vmem_limit_bytes=VMEM_LIMIT),
        name="rwkv",
    )(proj3, proj3, proj3, proj3, proj3, proj3, *params)


def _post_kernel(yf_ref, yb_ref, gate_ref, bonus_ref, att_ref, ga_ref, gb_ref, h_ref,
                 lng_ref, lnb_ref, wa_ref, wb_ref, wo_ref, o_ref):
    gmean = _head_sum_matrix(1.0 / HEAD_DIM)
    y = yf_ref[...] + yb_ref[...]
    mu = _head_sums(y, gmean)
    d = y - mu
    var = _head_sums(d * d, gmean)
    yn = d * lax.rsqrt(var + LNX_EPS) * lng_ref[...] + lnb_ref[...]
    out_a = ((yn + bonus_ref[...]) * gate_ref[...]).astype(bf16)
    ya = jnp.dot(out_a, wa_ref[...], preferred_element_type=f32)
    yb = jnp.dot(att_ref[...].astype(bf16), wb_ref[...], preferred_element_type=f32)
    merged = jax.nn.sigmoid(ga_ref[...]) * ya + jax.nn.sigmoid(gb_ref[...]) * yb
    o_ref[...] = h_ref[...] + jnp.dot(merged.astype(bf16), wo_ref[...], preferred_element_type=f32)


def _post(yf, yb, gate, bonus, att, proj3, x, lng, lnb, wa, wb, wo):
    b, n_tok, d = x.shape
    tm = ROW_TILE
    off = FRONT // tm
    row = lambda w: pl.BlockSpec((None, tm, w), lambda bi, i: (bi, i, 0))
    gate_cols = lambda cb: pl.BlockSpec((None, tm, d), lambda bi, i: (bi, i + off, cb))
    return pl.pallas_call(
        _post_kernel,
        grid=(b, n_tok // tm),
        in_specs=[row(RWKV_WIDTH), row(RWKV_WIDTH), row(RWKV_WIDTH), row(RWKV_WIDTH),
                  row(ATTN_Q_WIDTH), gate_cols(COL_GATE_A // d), gate_cols(COL_GATE_B // d), row(d),
                  _resident(lng, 2), _resident(lnb, 2), _resident(wa, 2), _resident(wb, 2),
                  _resident(wo, 2)],
        out_specs=row(d),
        out_shape=jax.ShapeDtypeStruct((b, n_tok, d), f32),
        compiler_params=pltpu.CompilerParams(
            dimension_semantics=("parallel", "parallel"), vmem_limit_bytes=VMEM_LIMIT),
        name="post",
    )(yf, yb, gate, bonus, att, proj3, proj3, x, lng, lnb, wa, wb, wo)


def _ffn_kernel(h_ref, g_ref, w1_ref, w2_ref, gf_ref, o_ref, *, ff_chunk):
    h = h_ref[...]
    ms = jnp.mean(h * h, axis=-1, keepdims=True)
    xn = (h * lax.rsqrt(ms + NORM_EPS) * g_ref[...]).astype(bf16)
    acc = h
    for c in range(0, w1_ref.shape[1], ff_chunk):
        f = jnp.dot(xn, w1_ref[:, c:c + ff_chunk], preferred_element_type=f32)
        f = jnp.square(jnp.maximum(f, 0.0)).astype(bf16)
        acc = acc + jnp.dot(f, w2_ref[c:c + ff_chunk, :], preferred_element_type=f32)
    ms2 = jnp.mean(acc * acc, axis=-1, keepdims=True)
    o_ref[...] = acc * lax.rsqrt(ms2 + NORM_EPS) * gf_ref[...]


def _ffn(h2d, g, w1, w2, gf):
    n, d = h2d.shape
    tm = _pick_tile(n, (1024, ROW_TILE))
    row = pl.BlockSpec((tm, d), lambda i: (i, 0))
    return pl.pallas_call(
        functools.partial(_ffn_kernel, ff_chunk=512),
        grid=(n // tm,),
        in_specs=[row, _resident(g, 1), _resident(w1, 1), _resident(w2, 1), _resident(gf, 1)],
        out_specs=row,
        out_shape=jax.ShapeDtypeStruct((n, d), f32),
        compiler_params=pltpu.CompilerParams(
            dimension_semantics=("parallel",), vmem_limit_bytes=VMEM_LIMIT),
        name="ffn",
    )(h2d, g, w1, w2, gf)


def _rope_tables(n_tok, lk):
    rows = n_tok // GRID_W
    inv_freq = ROPE_THETA ** (-jnp.arange(AXIS_FREQS, dtype=f32) * 2.0 / AXIS_DIM)
    row_ang = jnp.arange(rows, dtype=f32)[:, None] * inv_freq
    col_ang = jnp.arange(GRID_W, dtype=f32)[:, None] * inv_freq
    grid = jnp.stack([jnp.broadcast_to(row_ang[:, None, :], (rows, GRID_W, AXIS_FREQS)),
                      jnp.broadcast_to(col_ang[None, :, :], (rows, GRID_W, AXIS_FREQS))], axis=2)
    grid = grid.reshape(rows * GRID_W, 2, AXIS_FREQS)
    ang = jnp.concatenate([jnp.zeros((lk - n_tok, 2, AXIS_FREQS), f32), grid], axis=0)
    cos, sin = jnp.cos(ang), jnp.sin(ang)
    cos64 = jnp.stack([cos, cos], axis=2).reshape(lk, HEAD_DIM)
    sin64 = jnp.stack([-sin, sin], axis=2).reshape(lk, HEAD_DIM)
    return jnp.tile(cos64, (1, 2)), jnp.tile(sin64, (1, 2))


def kernel(x, meta_tokens, mix_norm_g, w_in, rwkv_shift, decay_w0, decay_w2, icl_a0, icl_a2, gate_w2, k_k, k_a, r_k, lnx_g, lnx_b, q_norm_g, k_norm_g, w_branch_rwkv, w_branch_attn, w_out, ffn_norm_g, w_ff1, w_ff2, final_norm_g):
    b, n_tok, d = x.shape
    assert n_tok % ROW_TILE == 0 and d == COL_GATE_B
    lp = FRONT + n_tok
    lk = lp - KEY_ROW0

    w = w_in[0]
    s0 = RWKV_IN
    s1 = s0 + ATTN_Q_WIDTH
    s2 = s1 + ATTN_KV_WIDTH
    s3 = s2 + ATTN_KV_WIDTH
    s4 = s3 + d
    w_perm = jnp.concatenate(
        [w[:, s3:s4], w[:, s4:], w[:, :s0], jnp.zeros((d, RWKV_IN_PAD - RWKV_IN), w.dtype),
         w[:, s0:s1], w[:, s1:s2], w[:, s2:s3]], axis=1).astype(bf16)
    mu = jnp.pad(rwkv_shift[0], ((0, 0), (0, RWKV_IN_PAD - RWKV_IN)))
    zl = jnp.zeros((DECAY_LORA, RWKV_WIDTH), f32)
    w2cat = jnp.concatenate([jnp.concatenate([decay_w2[0, 0], zl], axis=1),
                             jnp.concatenate([zl, decay_w2[0, 1]], axis=1)], axis=0)
    a2cat = jnp.concatenate([jnp.concatenate([icl_a2[0, 0], zl], axis=1),
                             jnp.concatenate([zl, icl_a2[0, 1]], axis=1)], axis=0)
    w0cat = decay_w0[0].reshape(1, 2 * RWKV_WIDTH)
    a0cat = icl_a0[0].reshape(1, 2 * RWKV_WIDTH)
    g2pad = jnp.pad(gate_w2[0], ((0, GD_PAD - GATE_LORA), (0, 0)))
    row = lambda a: a.reshape(1, -1)

    meta_frame = jnp.concatenate([jnp.zeros((META_ROW0, d), x.dtype), meta_tokens.astype(x.dtype)])
    proj3 = _in_proj(meta_frame, x, row(mix_norm_g[0]), w_perm)

    cos, sin = _rope_tables(n_tok, lk)
    qg = jnp.tile(row(q_norm_g[0]), (1, 2))
    kg = jnp.tile(row(k_norm_g[0]), (1, 2))
    q, k4, v4 = _qkv_prep(proj3, cos, sin, qg, kg)
    att = _attention(q, k4, v4, n_tok)

    yf, yb, gate, bonus = _rwkv(proj3, n_tok, mu, w0cat, w2cat, a0cat, a2cat, g2pad,
                                row(k_k[0]), row(k_a[0]), row(r_k[0]))

    h1 = _post(yf, yb, gate, bonus, att, proj3, x, row(lnx_g[0]), row(lnx_b[0]),
               w_branch_rwkv[0].astype(bf16), w_branch_attn[0].astype(bf16),
               w_out[0].astype(bf16))
    out = _ffn(h1.reshape(b * n_tok, d), row(ffn_norm_g[0]), w_ff1[0].astype(bf16),
               w_ff2[0].astype(bf16), row(final_norm_g))
    return out.reshape(b, n_tok, d)
```

```python
import functools

import jax
import jax.numpy as jnp
from jax import lax
from jax.experimental import pallas as pl
from jax.experimental.pallas import tpu as pltpu

f32 = jnp.float32
bf16 = jnp.bfloat16

N_META = 16
GRID_W = 64
HEAD_DIM = 64
RWKV_HEADS = 8
RWKV_WIDTH = RWKV_HEADS * HEAD_DIM
DECAY_LORA = 64
ICL_LORA = 64
GATE_LORA = 160
LNX_EPS = 64e-5
ATTN_Q_HEADS = 8
ATTN_KV_HEADS = 2
ATTN_GROUP = ATTN_Q_HEADS // ATTN_KV_HEADS
ATTN_Q_WIDTH = ATTN_Q_HEADS * HEAD_DIM
ATTN_KV_WIDTH = ATTN_KV_HEADS * HEAD_DIM
ROPE_THETA = 10000.0
AXIS_DIM = HEAD_DIM // 2
AXIS_FREQS = AXIS_DIM // 2
NORM_EPS = 1e-6

LANES = 128
CHUNK = 64
HALO = 8
PAIR = 2 * HEAD_DIM
N_PAIRS = RWKV_WIDTH // PAIR
ROW_TILE = 512
FRONT = ROW_TILE
META_ROW0 = FRONT - N_META
CHUNK0 = META_ROW0 // CHUNK
Q_TILE = 64
Q_SUB = 4
KEY_ROW0 = FRONT - LANES
LOG2E = 1.4426950408889634
MASK_BIAS = -1e30
VMEM_LIMIT = 56 * 1024 * 1024

RWKV_IN = 3 * RWKV_WIDTH + 2 * DECAY_LORA + 2 * ICL_LORA + GATE_LORA
RWKV_IN_PAD = 2048
COL_GATE_A = 0
COL_GATE_B = 1024
COL_RWKV = 2048
COL_Q = 4096
COL_K = 4608
COL_V = 4736
PROJ_W = 4864
OFF_R, OFF_K, OFF_V = 0, RWKV_WIDTH, 2 * RWKV_WIDTH
OFF_WD = 3 * RWKV_WIDTH
OFF_AD = OFF_WD + 2 * DECAY_LORA
OFF_GD = OFF_AD + 2 * ICL_LORA
GD_PAD = RWKV_IN_PAD - OFF_GD


def _pick_tile(n, candidates):
    for c in candidates:
        if n % c == 0:
            return c
    raise ValueError(f"no tile for {n} in {candidates}")


def _split3(x):
    h = x.astype(bf16)
    r = x - h.astype(f32)
    m = r.astype(bf16)
    l = (r - m.astype(f32)).astype(bf16)
    return h, m, l


def _dot(a, b, dims=None):
    if dims is None:
        return jnp.dot(a, b, preferred_element_type=f32)
    return lax.dot_general(a, b, (dims, ((), ())), preferred_element_type=f32)


def _mm(a, b):
    return _dot(a.astype(bf16), b.astype(bf16))


def _mm_exact_lhs(a_bf16, b):
    bh, bm, bl = _split3(b)
    return _dot(a_bf16, bh) + _dot(a_bf16, bm) + _dot(a_bf16, bl)


def _mm_exact_rhs(a, b_bf16):
    ah, am, al = _split3(a)
    return _dot(ah, b_bf16) + _dot(am, b_bf16) + _dot(al, b_bf16)


def _head_sum_matrix(scale):
    r = lax.broadcasted_iota(jnp.int32, (LANES, LANES), 0) // HEAD_DIM
    c = lax.broadcasted_iota(jnp.int32, (LANES, LANES), 1) // HEAD_DIM
    return jnp.where(r == c, scale, 0.0).astype(bf16)


def _head_sums(x, g):
    w = x.shape[-1]
    parts = [_mm_exact_rhs(x[:, i:i + LANES], g) for i in range(0, w, LANES)]
    return parts[0] if len(parts) == 1 else jnp.concatenate(parts, axis=-1)


def _resident(a, n_grid):
    zeros = (0,) * a.ndim
    return pl.BlockSpec(a.shape, lambda *_: zeros, pipeline_mode=pl.Buffered(1))


def _in_proj_kernel(mf_ref, x_ref, g_ref, w_ref, o_ref, u_ref):
    def norm_to(src_ref):
        x = src_ref[...]
        ms = jnp.mean(x * x, axis=-1, keepdims=True)
        u_ref[...] = (x * lax.rsqrt(ms + NORM_EPS) * g_ref[...]).astype(bf16)

    i = pl.program_id(1)
    pl.when(i == 0)(lambda: norm_to(mf_ref))
    pl.when(i > 0)(lambda: norm_to(x_ref))
    o_ref[...] = jnp.dot(u_ref[...], w_ref[...], preferred_element_type=f32)


def _in_proj(meta_frame, x, g, w):
    b, n_tok, d = x.shape
    nx = n_tok // ROW_TILE
    return pl.pallas_call(
        _in_proj_kernel,
        grid=(b, nx + 1),
        in_specs=[_resident(meta_frame, 2),
                  pl.BlockSpec((None, ROW_TILE, d), lambda bi, i: (bi, jnp.maximum(i - 1, 0), 0)),
                  _resident(g, 2), _resident(w, 2)],
        out_specs=pl.BlockSpec((None, ROW_TILE, PROJ_W), lambda bi, i: (bi, i, 0)),
        out_shape=jax.ShapeDtypeStruct((b, FRONT + n_tok, PROJ_W), f32),
        scratch_shapes=[pltpu.VMEM((ROW_TILE, d), bf16)],
        compiler_params=pltpu.CompilerParams(
            dimension_semantics=("parallel", "arbitrary"), vmem_limit_bytes=VMEM_LIMIT),
        name="in_proj",
    )(meta_frame, x, g, w)


def _norm_rope(x, gain, cos, sin_signed, gsum, scale):
    ms = _mm_exact_rhs(x * x, gsum)
    xn = x * lax.rsqrt(ms + NORM_EPS) * gain
    lane = lax.broadcasted_iota(jnp.int32, x.shape, 1)
    first_half = (lane % AXIS_DIM) < AXIS_FREQS
    partner = jnp.where(first_half,
                        pltpu.roll(xn, LANES - AXIS_FREQS, axis=1),
                        pltpu.roll(xn, AXIS_FREQS, axis=1))
    out = xn * cos + partner * sin_signed
    return out * scale if scale != 1.0 else out


def _q_kernel(q_ref, cos_ref, sin_ref, qg_ref, qo_ref):
    gsum = _head_sum_matrix(1.0 / HEAD_DIM)
    cos = cos_ref[...]
    sin = sin_ref[...]
    for i in range(0, ATTN_Q_WIDTH, LANES):
        qo_ref[:, i:i + LANES] = _norm_rope(q_ref[:, i:i + LANES], qg_ref[...], cos, sin, gsum,
                                            HEAD_DIM ** -0.5 * LOG2E).astype(bf16)


def _q_prep(proj3, n_tok, cos, sin, qg):
    b = proj3.shape[0]
    t = ROW_TILE
    off = FRONT // t
    tab = pl.BlockSpec((t, LANES), lambda bi, ti: (ti, 0))
    return pl.pallas_call(
        _q_kernel,
        grid=(b, n_tok // t),
        in_specs=[pl.BlockSpec((None, t, ATTN_Q_WIDTH),
                               lambda bi, ti: (bi, ti + off, COL_Q // ATTN_Q_WIDTH)),
                  tab, tab, pl.BlockSpec((1, LANES), lambda bi, ti: (0, 0))],
        out_specs=pl.BlockSpec((None, t, ATTN_Q_WIDTH), lambda bi, ti: (bi, ti, 0)),
        out_shape=jax.ShapeDtypeStruct((b, n_tok, ATTN_Q_WIDTH), bf16),
        compiler_params=pltpu.CompilerParams(dimension_semantics=("parallel", "parallel")),
        name="q_prep",
    )(proj3, cos, sin, qg)


def _kv_kernel(k_ref, v_ref, cos_ref, sin_ref, kg_ref, ko_ref, vo_ref):
    gsum = _head_sum_matrix(1.0 / HEAD_DIM)
    kn = _norm_rope(k_ref[...], kg_ref[...], cos_ref[...], sin_ref[...], gsum, 1.0)
    vf = v_ref[...]
    t = kn.shape[0]
    lane = lax.broadcasted_iota(jnp.int32, kn.shape, 1)
    key_row = lax.broadcasted_iota(jnp.int32, kn.shape, 0) + pl.program_id(1) * t
    extra = lane == HEAD_DIM
    k_aug = jnp.where(extra & (key_row < META_ROW0 - KEY_ROW0), MASK_BIAS, 0.0)
    v_aug = jnp.where(extra, 1.0, 0.0)
    for h in range(ATTN_KV_HEADS):
        kh = kn if h == 0 else pltpu.roll(kn, HEAD_DIM, axis=1)
        vh = vf if h == 0 else pltpu.roll(vf, HEAD_DIM, axis=1)
        ko_ref[h] = jnp.where(lane < HEAD_DIM, kh, k_aug).astype(bf16)
        vo_ref[h] = jnp.where(lane < HEAD_DIM, vh, v_aug).astype(bf16)


def _kv_prep(proj3, cos, sin, kg):
    b, lp, _ = proj3.shape
    lk = lp - KEY_ROW0
    t = _pick_tile(lk, (KEY_ROW0, LANES))
    off = KEY_ROW0 // t
    row = lambda w, cb: pl.BlockSpec((None, t, w), lambda bi, ti: (bi, ti + off, cb))
    tab = pl.BlockSpec((t, LANES), lambda bi, ti: (ti, 0))
    vec = pl.BlockSpec((1, LANES), lambda bi, ti: (0, 0))
    kv_out = pl.BlockSpec((None, ATTN_KV_HEADS, t, LANES), lambda bi, ti: (bi, 0, ti, 0))
    kv_shape = jax.ShapeDtypeStruct((b, ATTN_KV_HEADS, lk, LANES), bf16)
    return pl.pallas_call(
        _kv_kernel,
        grid=(b, lk // t),
        in_specs=[row(ATTN_KV_WIDTH, COL_K // ATTN_KV_WIDTH),
                  row(ATTN_KV_WIDTH, COL_V // ATTN_KV_WIDTH),
                  tab, tab, vec],
        out_specs=[kv_out, kv_out],
        out_shape=[kv_shape, kv_shape],
        compiler_params=pltpu.CompilerParams(dimension_semantics=("parallel", "parallel")),
        name="kv_prep",
    )(proj3, proj3, cos, sin, kg)


def _attn_kernel(q_ref, k_ref, v_ref, o_ref):
    tq = Q_TILE
    lane = lax.broadcasted_iota(jnp.int32, (tq, LANES), 1)
    one_hot = jnp.where(lane == HEAD_DIM, 1.0, 0.0)
    subs = range(q_ref.shape[0] // tq)

    def stacked_q(i):
        qf = q_ref[i * tq:(i + 1) * tq, :].astype(f32)
        rows = []
        for g in range(ATTN_GROUP):
            blk = qf[:, (g // 2) * LANES:(g // 2 + 1) * LANES]
            if g % 2:
                blk = pltpu.roll(blk, HEAD_DIM, axis=1)
            rows.append(jnp.where(lane < HEAD_DIM, blk, one_hot))
        return jnp.concatenate(rows, axis=0).astype(bf16)

    q = [stacked_q(i) for i in subs]
    s = [lax.dot_general(x, k_ref[...], (((1,), (1,)), ((), ())), preferred_element_type=f32)
         for x in q]
    m = [jnp.max(x, axis=-1, keepdims=True) for x in s]
    p = [jnp.exp2(x - y).astype(bf16) for x, y in zip(s, m)]
    ov = [jnp.dot(x, v_ref[...], preferred_element_type=f32) for x in p]
    for i in subs:
        on = ov[i] / ov[i][:, HEAD_DIM:HEAD_DIM + 1]
        for g in range(0, ATTN_GROUP, 2):
            even = on[g * tq:(g + 1) * tq]
            odd = pltpu.roll(on[(g + 1) * tq:(g + 2) * tq], HEAD_DIM, axis=1)
            o_ref[i * tq:(i + 1) * tq, (g // 2) * LANES:(g // 2 + 1) * LANES] = jnp.where(
                lane < HEAD_DIM, even, odd)


def _attention(q, k4, v4, n_tok):
    b, kvh, lk, n = k4.shape
    gw = ATTN_GROUP * HEAD_DIM
    tq = Q_TILE * Q_SUB
    return pl.pallas_call(
        _attn_kernel,
        grid=(b, kvh, n_tok // tq),
        in_specs=[pl.BlockSpec((None, tq, gw), lambda bi, hi, qi: (bi, qi, hi)),
                  pl.BlockSpec((None, None, lk, n), lambda bi, hi, qi: (bi, hi, 0, 0)),
                  pl.BlockSpec((None, None, lk, n), lambda bi, hi, qi: (bi, hi, 0, 0))],
        out_specs=pl.BlockSpec((None, tq, gw), lambda bi, hi, qi: (bi, qi, hi)),
        out_shape=jax.ShapeDtypeStruct((b, n_tok, ATTN_Q_WIDTH), f32),
        compiler_params=pltpu.CompilerParams(
            dimension_semantics=("parallel", "parallel", "parallel"), vmem_limit_bytes=VMEM_LIMIT),
        name="attention",
    )(q, k4, v4)


def _shifted_rows(p, prev_row, next_row):
    rows = lax.broadcasted_iota(jnp.int32, p.shape, 0)
    n = p.shape[0]
    prev = jnp.where(rows == 0, prev_row, pltpu.roll(p, 1, axis=0))
    nxt = jnp.where(rows == n - 1, next_row, pltpu.roll(p, n - 1, axis=0))
    return prev, nxt


def _stack_pair(x):
    lane = lax.broadcasted_iota(jnp.int32, x.shape, 1)
    first = lane < HEAD_DIM
    return jnp.concatenate([jnp.where(first, x, 0.0), jnp.where(first, 0.0, x)], axis=0)


def _chunk_terms(direction, p, prev_row, next_row, chunk_idx, prm):
    (mu_ref, w0_ref, w2_ref, a0_ref, a2_ref, g2_ref, kk_ref, ka_ref, rk_ref) = prm
    mu = mu_ref[...]
    prev, nxt = _shifted_rows(p, prev_row, next_row)
    z = p + mu[0:1] * (prev - p) + mu[1:2] * (nxt - p)

    rows = lax.broadcasted_iota(jnp.int32, (CHUNK, 1), 0) + chunk_idx * CHUNK
    valid = rows >= META_ROW0
    z = jnp.where(valid, z, 0.0)

    r = z[:, OFF_R:OFF_R + RWKV_WIDTH]
    k = z[:, OFF_K:OFF_K + RWKV_WIDTH]
    v = z[:, OFF_V:OFF_V + RWKV_WIDTH]
    lo, hi = direction * RWKV_WIDTH, (direction + 1) * RWKV_WIDTH

    x_w = w0_ref[...] + _mm(jnp.tanh(z[:, OFF_WD:OFF_AD]), w2_ref[...])
    logw = -jnp.exp(f32(-0.5)) * jax.nn.sigmoid(x_w[:, lo:hi])
    logw = jnp.where(valid, logw, 0.0)
    a_all = jax.nn.sigmoid(a0_ref[...] + _mm(z[:, OFF_AD:OFF_GD], a2_ref[...]))
    a_dir = a_all[:, lo:hi]

    gsum = _head_sum_matrix(1.0)
    kk = k * kk_ref[...]
    kk = kk / jnp.maximum(jnp.sqrt(_head_sums(kk * kk, gsum)), 1e-12)
    k_a = ka_ref[...]
    k_dir = k * (1.0 + (a_dir - 1.0) * k_a)
    kka = kk * a_dir

    extra = {}
    if direction == 0:
        extra["gate"] = _mm(jax.nn.sigmoid(z[:, OFF_GD:RWKV_IN_PAD]), g2_ref[...])
        a_mean = 0.5 * (a_all[:, :RWKV_WIDTH] + a_all[:, RWKV_WIDTH:])
        k_mean = k * (1.0 + (a_mean - 1.0) * k_a)
        extra["bonus"] = _head_sums(r * k_mean * rk_ref[...], gsum) * v

    tr = lax.broadcasted_iota(jnp.int32, (CHUNK, CHUNK), 0)
    tc = lax.broadcasted_iota(jnp.int32, (CHUNK, CHUNK), 1)
    tri = (tc <= tr) if direction == 0 else (tc >= tr)
    lc = _mm_exact_lhs(jnp.where(tri, 1.0, 0.0).astype(bf16), logw)
    last = CHUNK - 1 if direction == 0 else 0
    ltot = lc[last:last + 1, :]
    e_in = jnp.exp(lc)
    e_ex = jnp.exp(lc - logw)
    e_neg = jnp.exp(-lc)
    e_end = jnp.exp(ltot - lc)
    return dict(r=r * e_in, a=-kk * e_ex, b=kka * e_neg, k=k_dir * e_neg, v=v,
                be=kka * e_end, ke=k_dir * e_end, e_tot=jnp.exp(ltot), **extra)


def _scan_units(terms, h_ref, y_refs, live):
    sr = lax.broadcasted_iota(jnp.int32, (PAIR, PAIR), 0)
    sc = lax.broadcasted_iota(jnp.int32, (PAIR, PAIR), 1)
    same = (sr // CHUNK) == (sc // CHUNK)
    tt, ss = sr % CHUNK, sc % CHUNK
    strict = (same & (ss < tt), same & (ss > tt))
    incl = (same & (ss <= tt), same & (ss >= tt))
    eye = sr == sc
    eye_f = jnp.where(eye, 1.0, 0.0)
    levels = []
    m = 2
    while m < CHUNK:
        levels.append(((sr // (2 * m)) == (sc // (2 * m))) & ((sr // m) != (sc // m)))
        m *= 2
    base = (sr // 2) == (sc // 2)

    nt = ((1,), (1,))
    tn = ((0,), (0,))
    cat0 = lambda u, w: jnp.concatenate([u, w], axis=0)
    cat1 = lambda u, w: jnp.concatenate([u, w], axis=1)
    units = [(d, j) for d in range(2) for j in range(N_PAIRS)]
    sl = lambda j: slice(j * PAIR, (j + 1) * PAIR)
    stacked = lambda name, dt: [_stack_pair(terms[d][name][:, sl(j)]).astype(dt) for d, j in units]

    h0 = [h_ref[d, j] for d, j in units]
    a_s, b_s, k_s = stacked("a", bf16), stacked("b", bf16), stacked("k", bf16)
    v_s, be_s, ke_s = stacked("v", bf16), stacked("be", bf16), stacked("ke", bf16)
    r_s = stacked("r", f32)

    scores = [_dot(cat0(a_s[u], r_s[u].astype(bf16)), cat0(b_s[u], k_s[u]), nt)
              for u in range(len(units))]
    a_ab = [jnp.where(strict[d], scores[u][:PAIR, :PAIR], 0.0) for u, (d, _) in enumerate(units)]
    a_ak = [jnp.where(strict[d], scores[u][:PAIR, PAIR:], 0.0).astype(bf16)
            for u, (d, _) in enumerate(units)]
    a_rb = [jnp.where(incl[d], scores[u][PAIR:, :PAIR], 0.0).astype(bf16)
            for u, (d, _) in enumerate(units)]
    a_rk = [jnp.where(incl[d], scores[u][PAIR:, PAIR:], 0.0).astype(bf16)
            for u, (d, _) in enumerate(units)]

    t_inv = [eye_f + jnp.where(base, a, 0.0) for a in a_ab]
    for off in levels:
        tb = [t.astype(bf16) for t in t_inv]
        at = [_dot(jnp.where(off, a, 0.0).astype(bf16), t) for a, t in zip(a_ab, tb)]
        t_inv = [t + _dot(t16, x.astype(bf16)) for t, t16, x in zip(t_inv, tb, at)]

    av = [_dot(a, v) for a, v in zip(a_ak, v_s)]
    w2 = [_dot(t.astype(bf16), cat1(a, x.astype(bf16))).astype(bf16)
          for t, a, x in zip(t_inv, a_s, av)]
    qy = [_dot(a, w) for a, w in zip(a_rb, w2)]
    y0b = [_dot(a, v) for a, v in zip(a_rk, v_s)]
    mn = [_dot(b, w, tn) for b, w in zip(be_s, w2)]
    kv = [_dot(k, v, tn) for k, v in zip(ke_s, v_s)]
    qm = [_dot(cat0(r + x[:, :PAIR], y[:, :PAIR]).astype(bf16), h.astype(bf16))
          for r, x, y, h in zip(r_s, qy, mn, h0)]
    y_pair = []
    for u, (d, j) in enumerate(units):
        y_st = qm[u][:PAIR] + qy[u][:, PAIR:] + y0b[u]
        y_pair.append(y_st[:CHUNK] + y_st[CHUNK:])
        decay_col = jnp.sum(jnp.where(eye, terms[d]["e_tot"][:, sl(j)], 0.0), axis=1, keepdims=True)
        h_ref[d, j] = decay_col * h0[u] + qm[u][PAIR:] + mn[u][:, PAIR:] + kv[u]

    for d in range(2):
        @pl.when(live[d])
        def _(d=d):
            for u, (du, j) in enumerate(units):
                if du == d:
                    y_refs[d][:, sl(j)] = y_pair[u]


def _rwkv_kernel(pf_ref, pf_prev_ref, pf_next_ref, pb_ref, pb_prev_ref, pb_next_ref,
                 mu_ref, w0_ref, w2_ref, a0_ref, a2_ref, g2_ref, kk_ref, ka_ref, rk_ref,
                 yf_ref, yb_ref, gate_ref, bonus_ref, h_ref, *, n_chunks):
    c = pl.program_id(1)

    @pl.when(c == 0)
    def _():
        h_ref[...] = jnp.zeros_like(h_ref)

    prm = (mu_ref, w0_ref, w2_ref, a0_ref, a2_ref, g2_ref, kk_ref, ka_ref, rk_ref)
    terms, live = [], []
    for direction, (p_ref, prev_ref, next_ref) in enumerate(
            ((pf_ref, pf_prev_ref, pf_next_ref), (pb_ref, pb_prev_ref, pb_next_ref))):
        ci = c if direction == 0 else n_chunks - 1 - c
        prev_row = jnp.where(ci == 0, 0.0, prev_ref[HALO - 1:HALO, :])
        next_row = jnp.where(ci == n_chunks - 1, 0.0, next_ref[0:1, :])
        terms.append(_chunk_terms(direction, p_ref[...], prev_row, next_row, ci + CHUNK0, prm))
        live.append(ci > 0)

    @pl.when(live[0])
    def _():
        gate_ref[...] = terms[0]["gate"]
        bonus_ref[...] = terms[0]["bonus"]

    _scan_units(terms, h_ref, (yf_ref, yb_ref), live)


def _rwkv(proj3, n_tok, mu, w0, w2, a0, a2, g2, k_k, k_a, r_k):
    b, lp, _ = proj3.shape
    nc = lp // CHUNK - CHUNK0
    hb = CHUNK // HALO
    cb = COL_RWKV // RWKV_IN_PAD
    h_last = (lp // CHUNK) * hb - 1
    chunk_of = (lambda c: c + CHUNK0, lambda c: nc - 1 - c + CHUNK0)
    main = lambda d: pl.BlockSpec((None, CHUNK, RWKV_IN_PAD),
                                  lambda bi, c: (bi, chunk_of[d](c), cb))
    prev = lambda d: pl.BlockSpec((None, HALO, RWKV_IN_PAD),
                                  lambda bi, c: (bi, chunk_of[d](c) * hb - 1, cb))
    nxt = lambda d: pl.BlockSpec(
        (None, HALO, RWKV_IN_PAD),
        lambda bi, c: (bi, jnp.minimum((chunk_of[d](c) + 1) * hb, h_last), cb))
    out = lambda d: pl.BlockSpec(
        (None, CHUNK, RWKV_WIDTH),
        lambda bi, c: (bi, jnp.maximum(chunk_of[d](c) - CHUNK0 - 1, 0), 0))
    shp = jax.ShapeDtypeStruct((b, n_tok, RWKV_WIDTH), f32)
    params = (mu, w0, w2, a0, a2, g2, k_k, k_a, r_k)
    return pl.pallas_call(
        functools.partial(_rwkv_kernel, n_chunks=nc),
        grid=(b, nc),
        in_specs=[main(0), prev(0), nxt(0), main(1), prev(1), nxt(1)]
                 + [_resident(a, 2) for a in params],
        out_specs=[out(0), out(1), out(0), out(0)],
        out_shape=[shp, shp, shp, shp],
        scratch_shapes=[pltpu.VMEM((2, N_PAIRS, PAIR, PAIR), f32)],
        compiler_params=pltpu.CompilerParams(
            dimension_semantics=("parallel", "arbitrary"), vmem_limit_bytes=VMEM_LIMIT),
        name="rwkv",
    )(proj3, proj3, proj3, proj3, proj3, proj3, *params)


def _post_kernel(yf_ref, yb_ref, gate_ref, bonus_ref, att_ref, ga_ref, gb_ref, h_ref,
                 lng_ref, lnb_ref, wa_ref, wb_ref, wo_ref, o_ref):
    gmean = _head_sum_matrix(1.0 / HEAD_DIM)
    y = yf_ref[...] + yb_ref[...]
    mu = _head_sums(y, gmean)
    d = y - mu
    var = _head_sums(d * d, gmean)
    yn = d * lax.rsqrt(var + LNX_EPS) * lng_ref[...] + lnb_ref[...]
    out_a = ((yn + bonus_ref[...]) * gate_ref[...]).astype(bf16)
    ya = jnp.dot(out_a, wa_ref[...], preferred_element_type=f32)
    yb = jnp.dot(att_ref[...].astype(bf16), wb_ref[...], preferred_element_type=f32)
    merged = jax.nn.sigmoid(ga_ref[...]) * ya + jax.nn.sigmoid(gb_ref[...]) * yb
    o_ref[...] = h_ref[...] + jnp.dot(merged.astype(bf16), wo_ref[...], preferred_element_type=f32)


def _post(yf, yb, gate, bonus, att, proj3, x, lng, lnb, wa, wb, wo):
    b, n_tok, d = x.shape
    tm = ROW_TILE
    off = FRONT // tm
    row = lambda w: pl.BlockSpec((None, tm, w), lambda bi, i: (bi, i, 0))
    gate_cols = lambda cb: pl.BlockSpec((None, tm, d), lambda bi, i: (bi, i + off, cb))
    return pl.pallas_call(
        _post_kernel,
        grid=(b, n_tok // tm),
        in_specs=[row(RWKV_WIDTH), row(RWKV_WIDTH), row(RWKV_WIDTH), row(RWKV_WIDTH),
                  row(ATTN_Q_WIDTH), gate_cols(COL_GATE_A // d), gate_cols(COL_GATE_B // d), row(d),
                  _resident(lng, 2), _resident(lnb, 2), _resident(wa, 2), _resident(wb, 2),
                  _resident(wo, 2)],
        out_specs=row(d),
        out_shape=jax.ShapeDtypeStruct((b, n_tok, d), f32),
        compiler_params=pltpu.CompilerParams(
            dimension_semantics=("parallel", "parallel"), vmem_limit_bytes=VMEM_LIMIT),
        name="post",
    )(yf, yb, gate, bonus, att, proj3, proj3, x, lng, lnb, wa, wb, wo)


def _ffn_kernel(h_ref, g_ref, w1_ref, w2_ref, gf_ref, o_ref, *, ff_chunk):
    h = h_ref[...]
    ms = jnp.mean(h * h, axis=-1, keepdims=True)
    xn = (h * lax.rsqrt(ms + NORM_EPS) * g_ref[...]).astype(bf16)
    acc = h
    for c in range(0, w1_ref.shape[1], ff_chunk):
        f = jnp.dot(xn, w1_ref[:, c:c + ff_chunk], preferred_element_type=f32)
        f = jnp.square(jnp.maximum(f, 0.0)).astype(bf16)
        acc = acc + jnp.dot(f, w2_ref[c:c + ff_chunk, :], preferred_element_type=f32)
    ms2 = jnp.mean(acc * acc, axis=-1, keepdims=True)
    o_ref[...] = acc * lax.rsqrt(ms2 + NORM_EPS) * gf_ref[...]


def _ffn(h2d, g, w1, w2, gf):
    n, d = h2d.shape
    tm = _pick_tile(n, (1024, ROW_TILE))
    row = pl.BlockSpec((tm, d), lambda i: (i, 0))
    return pl.pallas_call(
        functools.partial(_ffn_kernel, ff_chunk=512),
        grid=(n // tm,),
        in_specs=[row, _resident(g, 1), _resident(w1, 1), _resident(w2, 1), _resident(gf, 1)],
        out_specs=row,
        out_shape=jax.ShapeDtypeStruct((n, d), f32),
        compiler_params=pltpu.CompilerParams(
            dimension_semantics=("parallel",), vmem_limit_bytes=VMEM_LIMIT),
        name="ffn",
    )(h2d, g, w1, w2, gf)


def _rope_tables(n_tok, lk):
    rows = n_tok // GRID_W
    inv_freq = ROPE_THETA ** (-jnp.arange(AXIS_FREQS, dtype=f32) * 2.0 / AXIS_DIM)
    row_ang = jnp.arange(rows, dtype=f32)[:, None] * inv_freq
    col_ang = jnp.arange(GRID_W, dtype=f32)[:, None] * inv_freq
    grid = jnp.stack([jnp.broadcast_to(row_ang[:, None, :], (rows, GRID_W, AXIS_FREQS)),
                      jnp.broadcast_to(col_ang[None, :, :], (rows, GRID_W, AXIS_FREQS))], axis=2)
    grid = grid.reshape(rows * GRID_W, 2, AXIS_FREQS)
    ang = jnp.concatenate([jnp.zeros((lk - n_tok, 2, AXIS_FREQS), f32), grid], axis=0)
    cos, sin = jnp.cos(ang), jnp.sin(ang)
    cos64 = jnp.stack([cos, cos], axis=2).reshape(lk, HEAD_DIM)
    sin64 = jnp.stack([-sin, sin], axis=2).reshape(lk, HEAD_DIM)
    return jnp.tile(cos64, (1, 2)), jnp.tile(sin64, (1, 2))


def kernel(x, meta_tokens, mix_norm_g, w_in, rwkv_shift, decay_w0, decay_w2, icl_a0, icl_a2, gate_w2, k_k, k_a, r_k, lnx_g, lnx_b, q_norm_g, k_norm_g, w_branch_rwkv, w_branch_attn, w_out, ffn_norm_g, w_ff1, w_ff2, final_norm_g):
    b, n_tok, d = x.shape
    assert n_tok % ROW_TILE == 0 and d == COL_GATE_B
    lp = FRONT + n_tok
    lk = lp - KEY_ROW0

    w = w_in[0]
    s0 = RWKV_IN
    s1 = s0 + ATTN_Q_WIDTH
    s2 = s1 + ATTN_KV_WIDTH
    s3 = s2 + ATTN_KV_WIDTH
    s4 = s3 + d
    w_perm = jnp.concatenate(
        [w[:, s3:s4], w[:, s4:], w[:, :s0], jnp.zeros((d, RWKV_IN_PAD - RWKV_IN), w.dtype),
         w[:, s0:s1], w[:, s1:s2], w[:, s2:s3]], axis=1).astype(bf16)
    mu = jnp.pad(rwkv_shift[0], ((0, 0), (0, RWKV_IN_PAD - RWKV_IN)))
    zl = jnp.zeros((DECAY_LORA, RWKV_WIDTH), f32)
    w2cat = jnp.concatenate([jnp.concatenate([decay_w2[0, 0], zl], axis=1),
                             jnp.concatenate([zl, decay_w2[0, 1]], axis=1)], axis=0)
    a2cat = jnp.concatenate([jnp.concatenate([icl_a2[0, 0], zl], axis=1),
                             jnp.concatenate([zl, icl_a2[0, 1]], axis=1)], axis=0)
    w0cat = decay_w0[0].reshape(1, 2 * RWKV_WIDTH)
    a0cat = icl_a0[0].reshape(1, 2 * RWKV_WIDTH)
    g2pad = jnp.pad(gate_w2[0], ((0, GD_PAD - GATE_LORA), (0, 0)))
    row = lambda a: a.reshape(1, -1)

    meta_frame = jnp.concatenate([jnp.zeros((META_ROW0, d), x.dtype), meta_tokens.astype(x.dtype)])
    proj3 = _in_proj(meta_frame, x, row(mix_norm_g[0]), w_perm)

    cos, sin = _rope_tables(n_tok, lk)
    qg = jnp.tile(row(q_norm_g[0]), (1, 2))
    kg = jnp.tile(row(k_norm_g[0]), (1, 2))
    tok0 = FRONT - KEY_ROW0
    q = _q_prep(proj3, n_tok, cos[tok0:], sin[tok0:], qg)
    k4, v4 = _kv_prep(proj3, cos, sin, kg)
    att = _attention(q, k4, v4, n_tok)

    yf, yb, gate, bonus = _rwkv(proj3, n_tok, mu, w0cat, w2cat, a0cat, a2cat, g2pad,
                                row(k_k[0]), row(k_a[0]), row(r_k[0]))

    h1 = _post(yf, yb, gate, bonus, att, proj3, x, row(lnx_g[0]), row(lnx_b[0]),
               w_branch_rwkv[0].astype(bf16), w_branch_attn[0].astype(bf16),
               w_out[0].astype(bf16))
    out = _ffn(h1.reshape(b * n_tok, d), row(ffn_norm_g[0]), w_ff1[0].astype(bf16),
               w_ff2[0].astype(bf16), row(final_norm_g))
    return out.reshape(b, n_tok, d)
```

```python
import functools

import jax
import jax.numpy as jnp
from jax import lax
from jax.experimental import pallas as pl
from jax.experimental.pallas import tpu as pltpu

f32 = jnp.float32
bf16 = jnp.bfloat16

N_META = 16
GRID_W = 64
HEAD_DIM = 64
RWKV_HEADS = 8
RWKV_WIDTH = RWKV_HEADS * HEAD_DIM
DECAY_LORA = 64
ICL_LORA = 64
GATE_LORA = 160
LNX_EPS = 64e-5
ATTN_Q_HEADS = 8
ATTN_KV_HEADS = 2
ATTN_GROUP = ATTN_Q_HEADS // ATTN_KV_HEADS
ATTN_Q_WIDTH = ATTN_Q_HEADS * HEAD_DIM
ATTN_KV_WIDTH = ATTN_KV_HEADS * HEAD_DIM
ROPE_THETA = 10000.0
AXIS_DIM = HEAD_DIM // 2
AXIS_FREQS = AXIS_DIM // 2
NORM_EPS = 1e-6

LANES = 128
CHUNK = 64
HALO = 8
PAIR = 2 * HEAD_DIM
N_PAIRS = RWKV_WIDTH // PAIR
ROW_TILE = 512
FRONT = ROW_TILE
META_ROW0 = FRONT - N_META
SCAN_SUB = 2
SCAN_ROWS = SCAN_SUB * CHUNK
BLOCK0 = META_ROW0 // SCAN_ROWS
Q_TILE = 64
Q_SUB = 4
KEY_ROW0 = FRONT - LANES
LOG2E = 1.4426950408889634
MASK_BIAS = -1e30
VMEM_LIMIT = 56 * 1024 * 1024

RWKV_IN = 3 * RWKV_WIDTH + 2 * DECAY_LORA + 2 * ICL_LORA + GATE_LORA
RWKV_IN_PAD = 2048
COL_GATE_A = 0
COL_GATE_B = 1024
COL_RWKV = 2048
COL_Q = 4096
COL_K = 4608
COL_V = 4736
PROJ_W = 4864
OFF_R, OFF_K, OFF_V = 0, RWKV_WIDTH, 2 * RWKV_WIDTH
OFF_WD = 3 * RWKV_WIDTH
OFF_AD = OFF_WD + 2 * DECAY_LORA
OFF_GD = OFF_AD + 2 * ICL_LORA
GD_PAD = RWKV_IN_PAD - OFF_GD


def _pick_tile(n, candidates):
    for c in candidates:
        if n % c == 0:
            return c
    raise ValueError(f"no tile for {n} in {candidates}")


def _split3(x):
    h = x.astype(bf16)
    r = x - h.astype(f32)
    m = r.astype(bf16)
    l = (r - m.astype(f32)).astype(bf16)
    return h, m, l


def _dot(a, b, dims=None):
    if dims is None:
        return jnp.dot(a, b, preferred_element_type=f32)
    return lax.dot_general(a, b, (dims, ((), ())), preferred_element_type=f32)


def _mm(a, b):
    return _dot(a.astype(bf16), b.astype(bf16))


def _mm_exact_lhs(a_bf16, b):
    bh, bm, bl = _split3(b)
    return _dot(a_bf16, bh) + _dot(a_bf16, bm) + _dot(a_bf16, bl)


def _mm_exact_rhs(a, b_bf16):
    ah, am, al = _split3(a)
    return _dot(ah, b_bf16) + _dot(am, b_bf16) + _dot(al, b_bf16)


def _head_sum_matrix(scale):
    r = lax.broadcasted_iota(jnp.int32, (LANES, LANES), 0) // HEAD_DIM
    c = lax.broadcasted_iota(jnp.int32, (LANES, LANES), 1) // HEAD_DIM
    return jnp.where(r == c, scale, 0.0).astype(bf16)


def _head_sums(x, g):
    w = x.shape[-1]
    parts = [_mm_exact_rhs(x[:, i:i + LANES], g) for i in range(0, w, LANES)]
    return parts[0] if len(parts) == 1 else jnp.concatenate(parts, axis=-1)


def _resident(a, n_grid):
    zeros = (0,) * a.ndim
    return pl.BlockSpec(a.shape, lambda *_: zeros, pipeline_mode=pl.Buffered(1))


def _in_proj_kernel(mf_ref, x_ref, g_ref, w_ref, o_ref, u_ref):
    def norm_to(src_ref):
        x = src_ref[...]
        ms = jnp.mean(x * x, axis=-1, keepdims=True)
        u_ref[...] = (x * lax.rsqrt(ms + NORM_EPS) * g_ref[...]).astype(bf16)

    i = pl.program_id(1)
    pl.when(i == 0)(lambda: norm_to(mf_ref))
    pl.when(i > 0)(lambda: norm_to(x_ref))
    o_ref[...] = jnp.dot(u_ref[...], w_ref[...], preferred_element_type=f32)


def _in_proj(meta_frame, x, g, w):
    b, n_tok, d = x.shape
    nx = n_tok // ROW_TILE
    return pl.pallas_call(
        _in_proj_kernel,
        grid=(b, nx + 1),
        in_specs=[_resident(meta_frame, 2),
                  pl.BlockSpec((None, ROW_TILE, d), lambda bi, i: (bi, jnp.maximum(i - 1, 0), 0)),
                  _resident(g, 2), _resident(w, 2)],
        out_specs=pl.BlockSpec((None, ROW_TILE, PROJ_W), lambda bi, i: (bi, i, 0)),
        out_shape=jax.ShapeDtypeStruct((b, FRONT + n_tok, PROJ_W), f32),
        scratch_shapes=[pltpu.VMEM((ROW_TILE, d), bf16)],
        compiler_params=pltpu.CompilerParams(
            dimension_semantics=("parallel", "arbitrary"), vmem_limit_bytes=VMEM_LIMIT),
        name="in_proj",
    )(meta_frame, x, g, w)


def _norm_rope(x, gain, cos, sin_signed, gsum, scale):
    ms = _mm_exact_rhs(x * x, gsum)
    xn = x * lax.rsqrt(ms + NORM_EPS) * gain
    lane = lax.broadcasted_iota(jnp.int32, x.shape, 1)
    first_half = (lane % AXIS_DIM) < AXIS_FREQS
    partner = jnp.where(first_half,
                        pltpu.roll(xn, LANES - AXIS_FREQS, axis=1),
                        pltpu.roll(xn, AXIS_FREQS, axis=1))
    out = xn * cos + partner * sin_signed
    return out * scale if scale != 1.0 else out


def _q_kernel(q_ref, cos_ref, sin_ref, qg_ref, qo_ref):
    gsum = _head_sum_matrix(1.0 / HEAD_DIM)
    cos = cos_ref[...]
    sin = sin_ref[...]
    for i in range(0, ATTN_Q_WIDTH, LANES):
        qo_ref[:, i:i + LANES] = _norm_rope(q_ref[:, i:i + LANES], qg_ref[...], cos, sin, gsum,
                                            HEAD_DIM ** -0.5 * LOG2E).astype(bf16)


def _q_prep(proj3, n_tok, cos, sin, qg):
    b = proj3.shape[0]
    t = ROW_TILE
    off = FRONT // t
    tab = pl.BlockSpec((t, LANES), lambda bi, ti: (ti, 0))
    return pl.pallas_call(
        _q_kernel,
        grid=(b, n_tok // t),
        in_specs=[pl.BlockSpec((None, t, ATTN_Q_WIDTH),
                               lambda bi, ti: (bi, ti + off, COL_Q // ATTN_Q_WIDTH)),
                  tab, tab, pl.BlockSpec((1, LANES), lambda bi, ti: (0, 0))],
        out_specs=pl.BlockSpec((None, t, ATTN_Q_WIDTH), lambda bi, ti: (bi, ti, 0)),
        out_shape=jax.ShapeDtypeStruct((b, n_tok, ATTN_Q_WIDTH), bf16),
        compiler_params=pltpu.CompilerParams(dimension_semantics=("parallel", "parallel")),
        name="q_prep",
    )(proj3, cos, sin, qg)


def _kv_kernel(k_ref, v_ref, cos_ref, sin_ref, kg_ref, ko_ref, vo_ref):
    gsum = _head_sum_matrix(1.0 / HEAD_DIM)
    kn = _norm_rope(k_ref[...], kg_ref[...], cos_ref[...], sin_ref[...], gsum, 1.0)
    vf = v_ref[...]
    t = kn.shape[0]
    lane = lax.broadcasted_iota(jnp.int32, kn.shape, 1)
    key_row = lax.broadcasted_iota(jnp.int32, kn.shape, 0) + pl.program_id(1) * t
    extra = lane == HEAD_DIM
    k_aug = jnp.where(extra & (key_row < META_ROW0 - KEY_ROW0), MASK_BIAS, 0.0)
    v_aug = jnp.where(extra, 1.0, 0.0)
    for h in range(ATTN_KV_HEADS):
        kh = kn if h == 0 else pltpu.roll(kn, HEAD_DIM, axis=1)
        vh = vf if h == 0 else pltpu.roll(vf, HEAD_DIM, axis=1)
        ko_ref[h] = jnp.where(lane < HEAD_DIM, kh, k_aug).astype(bf16)
        vo_ref[h] = jnp.where(lane < HEAD_DIM, vh, v_aug).astype(bf16)


def _kv_prep(proj3, cos, sin, kg):
    b, lp, _ = proj3.shape
    lk = lp - KEY_ROW0
    t = _pick_tile(lk, (KEY_ROW0, LANES))
    off = KEY_ROW0 // t
    row = lambda w, cb: pl.BlockSpec((None, t, w), lambda bi, ti: (bi, ti + off, cb))
    tab = pl.BlockSpec((t, LANES), lambda bi, ti: (ti, 0))
    vec = pl.BlockSpec((1, LANES), lambda bi, ti: (0, 0))
    kv_out = pl.BlockSpec((None, ATTN_KV_HEADS, t, LANES), lambda bi, ti: (bi, 0, ti, 0))
    kv_shape = jax.ShapeDtypeStruct((b, ATTN_KV_HEADS, lk, LANES), bf16)
    return pl.pallas_call(
        _kv_kernel,
        grid=(b, lk // t),
        in_specs=[row(ATTN_KV_WIDTH, COL_K // ATTN_KV_WIDTH),
                  row(ATTN_KV_WIDTH, COL_V // ATTN_KV_WIDTH),
                  tab, tab, vec],
        out_specs=[kv_out, kv_out],
        out_shape=[kv_shape, kv_shape],
        compiler_params=pltpu.CompilerParams(dimension_semantics=("parallel", "parallel")),
        name="kv_prep",
    )(proj3, proj3, cos, sin, kg)


def _attn_kernel(q_ref, k_ref, v_ref, o_ref):
    tq = Q_TILE
    lane = lax.broadcasted_iota(jnp.int32, (tq, LANES), 1)
    one_hot = jnp.where(lane == HEAD_DIM, 1.0, 0.0)
    subs = range(q_ref.shape[0] // tq)

    def stacked_q(i):
        qf = q_ref[i * tq:(i + 1) * tq, :].astype(f32)
        rows = []
        for g in range(ATTN_GROUP):
            blk = qf[:, (g // 2) * LANES:(g // 2 + 1) * LANES]
            if g % 2:
                blk = pltpu.roll(blk, HEAD_DIM, axis=1)
            rows.append(jnp.where(lane < HEAD_DIM, blk, one_hot))
        return jnp.concatenate(rows, axis=0).astype(bf16)

    q = [stacked_q(i) for i in subs]
    s = [lax.dot_general(x, k_ref[...], (((1,), (1,)), ((), ())), preferred_element_type=f32)
         for x in q]
    m = [jnp.max(x, axis=-1, keepdims=True) for x in s]
    p = [jnp.exp2(x - y).astype(bf16) for x, y in zip(s, m)]
    ov = [jnp.dot(x, v_ref[...], preferred_element_type=f32) for x in p]
    for i in subs:
        on = ov[i] / ov[i][:, HEAD_DIM:HEAD_DIM + 1]
        for g in range(0, ATTN_GROUP, 2):
            even = on[g * tq:(g + 1) * tq]
            odd = pltpu.roll(on[(g + 1) * tq:(g + 2) * tq], HEAD_DIM, axis=1)
            o_ref[i * tq:(i + 1) * tq, (g // 2) * LANES:(g // 2 + 1) * LANES] = jnp.where(
                lane < HEAD_DIM, even, odd)


def _attention(q, k4, v4, n_tok):
    b, kvh, lk, n = k4.shape
    gw = ATTN_GROUP * HEAD_DIM
    tq = Q_TILE * Q_SUB
    return pl.pallas_call(
        _attn_kernel,
        grid=(b, kvh, n_tok // tq),
        in_specs=[pl.BlockSpec((None, tq, gw), lambda bi, hi, qi: (bi, qi, hi)),
                  pl.BlockSpec((None, None, lk, n), lambda bi, hi, qi: (bi, hi, 0, 0)),
                  pl.BlockSpec((None, None, lk, n), lambda bi, hi, qi: (bi, hi, 0, 0))],
        out_specs=pl.BlockSpec((None, tq, gw), lambda bi, hi, qi: (bi, qi, hi)),
        out_shape=jax.ShapeDtypeStruct((b, n_tok, ATTN_Q_WIDTH), f32),
        compiler_params=pltpu.CompilerParams(
            dimension_semantics=("parallel", "parallel", "parallel"), vmem_limit_bytes=VMEM_LIMIT),
        name="attention",
    )(q, k4, v4)


def _shifted_rows(p, prev_row, next_row):
    rows = lax.broadcasted_iota(jnp.int32, p.shape, 0)
    n = p.shape[0]
    prev = jnp.where(rows == 0, prev_row, pltpu.roll(p, 1, axis=0))
    nxt = jnp.where(rows == n - 1, next_row, pltpu.roll(p, n - 1, axis=0))
    return prev, nxt


def _stack_pair(x):
    lane = lax.broadcasted_iota(jnp.int32, x.shape, 1)
    first = lane < HEAD_DIM
    return jnp.concatenate([jnp.where(first, x, 0.0), jnp.where(first, 0.0, x)], axis=0)


STASH_NAMES = ("a", "b", "k", "v", "be", "ke")


def _chunk_terms(direction, p, prev_row, next_row, chunk_idx, prm, out):
    (mu_ref, w0_ref, w2_ref, a0_ref, a2_ref, g2_ref, kk_ref, ka_ref, rk_ref) = prm
    mu = mu_ref[...]
    prev, nxt = _shifted_rows(p, prev_row, next_row)
    z = p + mu[0:1] * (prev - p) + mu[1:2] * (nxt - p)

    rows = lax.broadcasted_iota(jnp.int32, (SCAN_ROWS, 1), 0) + chunk_idx * SCAN_ROWS
    valid = rows >= META_ROW0
    z = jnp.where(valid, z, 0.0)
    yield

    r = z[:, OFF_R:OFF_R + RWKV_WIDTH]
    k = z[:, OFF_K:OFF_K + RWKV_WIDTH]
    v = z[:, OFF_V:OFF_V + RWKV_WIDTH]
    lo, hi = direction * RWKV_WIDTH, (direction + 1) * RWKV_WIDTH

    x_w = w0_ref[...] + _mm(jnp.tanh(z[:, OFF_WD:OFF_AD]), w2_ref[...])
    logw = -jnp.exp(f32(-0.5)) * jax.nn.sigmoid(x_w[:, lo:hi])
    logw = jnp.where(valid, logw, 0.0)
    a_all = jax.nn.sigmoid(a0_ref[...] + _mm(z[:, OFF_AD:OFF_GD], a2_ref[...]))
    a_dir = a_all[:, lo:hi]
    yield

    gsum = _head_sum_matrix(1.0)
    kk = k * kk_ref[...]
    kk = kk / jnp.maximum(jnp.sqrt(_head_sums(kk * kk, gsum)), 1e-12)
    k_a = ka_ref[...]
    k_dir = k * (1.0 + (a_dir - 1.0) * k_a)
    kka = kk * a_dir
    yield

    if direction == 0:
        out["gate"] = _mm(jax.nn.sigmoid(z[:, OFF_GD:RWKV_IN_PAD]), g2_ref[...])
        a_mean = 0.5 * (a_all[:, :RWKV_WIDTH] + a_all[:, RWKV_WIDTH:])
        k_mean = k * (1.0 + (a_mean - 1.0) * k_a)
        out["bonus"] = _head_sums(r * k_mean * rk_ref[...], gsum) * v
        yield

    tr = lax.broadcasted_iota(jnp.int32, (SCAN_ROWS, SCAN_ROWS), 0)
    tc = lax.broadcasted_iota(jnp.int32, (SCAN_ROWS, SCAN_ROWS), 1)
    tri = ((tr // CHUNK) == (tc // CHUNK)) & ((tc <= tr) if direction == 0 else (tc >= tr))
    lc = _mm_exact_lhs(jnp.where(tri, 1.0, 0.0).astype(bf16), logw)
    last = CHUNK - 1 if direction == 0 else 0
    ltots = [lc[c * CHUNK + last:c * CHUNK + last + 1, :] for c in range(SCAN_SUB)]
    out["e_tot"] = [jnp.exp(t) for t in ltots]
    ltot = jnp.concatenate([jnp.broadcast_to(t, (CHUNK, RWKV_WIDTH)) for t in ltots], axis=0)
    yield

    sl = lambda j: slice(j * PAIR, (j + 1) * PAIR)
    stack = lambda x, dt: [[_stack_pair(x[c * CHUNK:(c + 1) * CHUNK, sl(j)]).astype(dt)
                            for j in range(N_PAIRS)] for c in range(SCAN_SUB)]
    e_neg = jnp.exp(-lc)
    out["b"] = stack(kka * e_neg, bf16)
    yield
    out["k"] = stack(k_dir * e_neg, bf16)
    yield
    e_end = jnp.exp(ltot - lc)
    out["be"] = stack(kka * e_end, bf16)
    yield
    out["ke"] = stack(k_dir * e_end, bf16)
    yield
    out["a"] = stack(-kk * jnp.exp(lc - logw), bf16)
    yield
    out["r"] = stack(r * jnp.exp(lc), f32)
    yield
    out["v"] = stack(v, bf16)


def _scan_units(units, ops, r_s, e_tot, h0, out):
    sr = lax.broadcasted_iota(jnp.int32, (PAIR, PAIR), 0)
    sc = lax.broadcasted_iota(jnp.int32, (PAIR, PAIR), 1)
    same = (sr // CHUNK) == (sc // CHUNK)
    tt, ss = sr % CHUNK, sc % CHUNK
    strict = (same & (ss < tt), same & (ss > tt))
    incl = (same & (ss <= tt), same & (ss >= tt))
    eye = sr == sc
    eye_f = jnp.where(eye, 1.0, 0.0)
    levels = []
    m = 2
    while m < CHUNK:
        levels.append(((sr // (2 * m)) == (sc // (2 * m))) & ((sr // m) != (sc // m)))
        m *= 2
    base = (sr // 2) == (sc // 2)

    nt = ((1,), (1,))
    tn = ((0,), (0,))
    cat0 = lambda u, w: jnp.concatenate([u, w], axis=0)
    cat1 = lambda u, w: jnp.concatenate([u, w], axis=1)
    sl = lambda j: slice(j * PAIR, (j + 1) * PAIR)
    a_s, b_s, k_s, v_s, be_s, ke_s = (ops[n] for n in STASH_NAMES)

    a_ab, a_ak, a_rb, a_rk = [], [], [], []
    for u, (d, _, _) in enumerate(units):
        scores = _dot(cat0(a_s[u], r_s[u].astype(bf16)), cat0(b_s[u], k_s[u]), nt)
        a_ab.append(jnp.where(strict[d], scores[:PAIR, :PAIR], 0.0))
        a_ak.append(jnp.where(strict[d], scores[:PAIR, PAIR:], 0.0).astype(bf16))
        a_rb.append(jnp.where(incl[d], scores[PAIR:, :PAIR], 0.0).astype(bf16))
        a_rk.append(jnp.where(incl[d], scores[PAIR:, PAIR:], 0.0).astype(bf16))
    yield

    t_inv = [eye_f + jnp.where(base, a, 0.0) for a in a_ab]

    akv = [_dot(cat0(x, y), v) for x, y, v in zip(a_ak, a_rk, v_s)]
    av = [x[:PAIR] for x in akv]
    y0b = [x[PAIR:] for x in akv]
    kv = [_dot(k, v, tn) for k, v in zip(ke_s, v_s)]
    yield
    for off in levels:
        tb = [t.astype(bf16) for t in t_inv]
        at = [_dot(jnp.where(off, a, 0.0).astype(bf16), t) for a, t in zip(a_ab, tb)]
        yield
        t_inv = [t + _dot(t16, x.astype(bf16)) for t, t16, x in zip(t_inv, tb, at)]
        yield

    w2 = [_dot(t.astype(bf16), cat1(a, x.astype(bf16))).astype(bf16)
          for t, a, x in zip(t_inv, a_s, av)]
    yield
    qy = [_dot(a, w) for a, w in zip(a_rb, w2)]
    mn = [_dot(b, w, tn) for b, w in zip(be_s, w2)]
    yield
    qh = [cat0(r + x[:, :PAIR], y[:, :PAIR]).astype(bf16) for r, x, y in zip(r_s, qy, mn)]
    yield
    out["y"], out["h"] = {}, dict(h0)
    for pos in range(SCAN_SUB):
        for u, (d, c, j) in enumerate(units):
            if c != (pos if d == 0 else SCAN_SUB - 1 - pos):
                continue
            h = out["h"][d, j]
            qm = _dot(qh[u], h.astype(bf16))
            y_st = qm[:PAIR] + qy[u][:, PAIR:] + y0b[u]
            out["y"][d, c, j] = y_st[:CHUNK] + y_st[CHUNK:]
            decay_col = jnp.sum(jnp.where(eye, e_tot[d][c][:, sl(j)], 0.0), axis=1, keepdims=True)
            out["h"][d, j] = decay_col * h + qm[PAIR:] + mn[u][:, PAIR:] + kv[u]
        yield


def _rwkv_kernel(pf_ref, pf_prev_ref, pf_next_ref, pb_ref, pb_prev_ref, pb_next_ref,
                 mu_ref, w0_ref, w2_ref, a0_ref, a2_ref, g2_ref, kk_ref, ka_ref, rk_ref,
                 yf_ref, yb_ref, gate_ref, bonus_ref,
                 h_ref, ops_ref, r_ref, e_ref, *, n_blocks):
    s = pl.program_id(1)

    @pl.when(s == 0)
    def _():
        ops_ref[...] = jnp.zeros_like(ops_ref)
        r_ref[...] = jnp.zeros_like(r_ref)
        e_ref[...] = jnp.zeros_like(e_ref)

    @pl.when(s <= 1)
    def _():
        h_ref[...] = jnp.zeros_like(h_ref)

    units = [(d, c, j) for d in range(2) for c in range(SCAN_SUB) for j in range(N_PAIRS)]
    sl = lambda j: slice(j * PAIR, (j + 1) * PAIR)
    rows = lambda c: slice(c * CHUNK, (c + 1) * CHUNK)

    ops = {n: [ops_ref[d, i, c, j] for d, c, j in units] for i, n in enumerate(STASH_NAMES)}
    e_tot = [[e_ref[d, c] for c in range(SCAN_SUB)] for d in range(2)]
    h0 = {(d, j): h_ref[d, j] for d in range(2) for j in range(N_PAIRS)}
    scan = {}
    scan_gen = _scan_units(units, ops, [r_ref[d, c, j] for d, c, j in units], e_tot, h0, scan)

    prm = (mu_ref, w0_ref, w2_ref, a0_ref, a2_ref, g2_ref, kk_ref, ka_ref, rk_ref)
    terms = [{}, {}]
    stages = [scan_gen]
    for direction, (p_ref, prev_ref, next_ref) in enumerate(
            ((pf_ref, pf_prev_ref, pf_next_ref), (pb_ref, pb_prev_ref, pb_next_ref))):
        bi = jnp.minimum(s, n_blocks - 1) if direction == 0 else jnp.maximum(n_blocks - 1 - s, 0)
        prev_row = jnp.where(bi == 0, 0.0, prev_ref[HALO - 1:HALO, :])
        next_row = jnp.where(bi == n_blocks - 1, 0.0, next_ref[0:1, :])
        stages.append(_chunk_terms(direction, p_ref[...], prev_row, next_row, bi + BLOCK0, prm,
                                   terms[direction]))
    while stages:
        for g in list(stages):
            if next(g, stages) is stages:
                stages.remove(g)

    for (d, j), h in scan["h"].items():
        h_ref[d, j] = h
    for d, c, j in units:
        for i, n in enumerate(STASH_NAMES):
            ops_ref[d, i, c, j] = terms[d][n][c][j]
        r_ref[d, c, j] = terms[d]["r"][c][j]
    for d in range(2):
        for c in range(SCAN_SUB):
            e_ref[d, c] = terms[d]["e_tot"][c]

    @pl.when((s >= 1) & (s <= n_blocks - 1))
    def _():
        gate_ref[...] = terms[0]["gate"]
        bonus_ref[...] = terms[0]["bonus"]

    @pl.when(s >= 2)
    def _():
        for d, c, j in units:
            if d == 0:
                yf_ref[rows(c), sl(j)] = scan["y"][d, c, j]

    @pl.when((s >= 1) & (s <= n_blocks - 1))
    def _():
        for d, c, j in units:
            if d == 1:
                yb_ref[rows(c), sl(j)] = scan["y"][d, c, j]


def _rwkv(proj3, n_tok, mu, w0, w2, a0, a2, g2, k_k, k_a, r_k):
    b, lp, _ = proj3.shape
    nb = lp // SCAN_ROWS - BLOCK0
    hb = SCAN_ROWS // HALO
    cb = COL_RWKV // RWKV_IN_PAD
    h_last = (lp // SCAN_ROWS) * hb - 1
    block_of = (lambda s: jnp.minimum(s, nb - 1) + BLOCK0,
                lambda s: jnp.maximum(nb - 1 - s, 0) + BLOCK0)
    main = lambda d: pl.BlockSpec((None, SCAN_ROWS, RWKV_IN_PAD),
                                  lambda bi, s: (bi, block_of[d](s), cb))
    prev = lambda d: pl.BlockSpec((None, HALO, RWKV_IN_PAD),
                                  lambda bi, s: (bi, block_of[d](s) * hb - 1, cb))
    nxt = lambda d: pl.BlockSpec(
        (None, HALO, RWKV_IN_PAD),
        lambda bi, s: (bi, jnp.minimum((block_of[d](s) + 1) * hb, h_last), cb))
    blk = lambda f: pl.BlockSpec((None, SCAN_ROWS, RWKV_WIDTH), lambda bi, s: (bi, f(s), 0))
    y_fwd = blk(lambda s: jnp.maximum(s - 2, 0))
    y_bwd = blk(lambda s: jnp.clip(nb - s - 1, 0, nb - 2))
    prep = blk(lambda s: jnp.clip(s - 1, 0, nb - 2))
    shp = jax.ShapeDtypeStruct((b, n_tok, RWKV_WIDTH), f32)
    params = (mu, w0, w2, a0, a2, g2, k_k, k_a, r_k)
    return pl.pallas_call(
        functools.partial(_rwkv_kernel, n_blocks=nb),
        grid=(b, nb + 1),
        in_specs=[main(0), prev(0), nxt(0), main(1), prev(1), nxt(1)]
                 + [_resident(a, 2) for a in params],
        out_specs=[y_fwd, y_bwd, prep, prep],
        out_shape=[shp, shp, shp, shp],
        scratch_shapes=[pltpu.VMEM((2, N_PAIRS, PAIR, PAIR), f32),
                        pltpu.VMEM((2, len(STASH_NAMES), SCAN_SUB, N_PAIRS, PAIR, PAIR), bf16),
                        pltpu.VMEM((2, SCAN_SUB, N_PAIRS, PAIR, PAIR), f32),
                        pltpu.VMEM((2, SCAN_SUB, 1, RWKV_WIDTH), f32)],
        compiler_params=pltpu.CompilerParams(
            dimension_semantics=("parallel", "arbitrary"), vmem_limit_bytes=VMEM_LIMIT),
        name="rwkv",
    )(proj3, proj3, proj3, proj3, proj3, proj3, *params)


def _post_kernel(yf_ref, yb_ref, gate_ref, bonus_ref, att_ref, ga_ref, gb_ref, h_ref,
                 lng_ref, lnb_ref, wa_ref, wb_ref, wo_ref, o_ref):
    gmean = _head_sum_matrix(1.0 / HEAD_DIM)
    y = yf_ref[...] + yb_ref[...]
    mu = _head_sums(y, gmean)
    d = y - mu
    var = _head_sums(d * d, gmean)
    yn = d * lax.rsqrt(var + LNX_EPS) * lng_ref[...] + lnb_ref[...]
    out_a = ((yn + bonus_ref[...]) * gate_ref[...]).astype(bf16)
    ya = jnp.dot(out_a, wa_ref[...], preferred_element_type=f32)
    yb = jnp.dot(att_ref[...].astype(bf16), wb_ref[...], preferred_element_type=f32)
    merged = jax.nn.sigmoid(ga_ref[...]) * ya + jax.nn.sigmoid(gb_ref[...]) * yb
    o_ref[...] = h_ref[...] + jnp.dot(merged.astype(bf16), wo_ref[...], preferred_element_type=f32)


def _post(yf, yb, gate, bonus, att, proj3, x, lng, lnb, wa, wb, wo):
    b, n_tok, d = x.shape
    tm = ROW_TILE
    off = FRONT // tm
    row = lambda w: pl.BlockSpec((None, tm, w), lambda bi, i: (bi, i, 0))
    gate_cols = lambda cb: pl.BlockSpec((None, tm, d), lambda bi, i: (bi, i + off, cb))
    return pl.pallas_call(
        _post_kernel,
        grid=(b, n_tok // tm),
        in_specs=[row(RWKV_WIDTH), row(RWKV_WIDTH), row(RWKV_WIDTH), row(RWKV_WIDTH),
                  row(ATTN_Q_WIDTH), gate_cols(COL_GATE_A // d), gate_cols(COL_GATE_B // d), row(d),
                  _resident(lng, 2), _resident(lnb, 2), _resident(wa, 2), _resident(wb, 2),
                  _resident(wo, 2)],
        out_specs=row(d),
        out_shape=jax.ShapeDtypeStruct((b, n_tok, d), f32),
        compiler_params=pltpu.CompilerParams(
            dimension_semantics=("parallel", "parallel"), vmem_limit_bytes=VMEM_LIMIT),
        name="post",
    )(yf, yb, gate, bonus, att, proj3, proj3, x, lng, lnb, wa, wb, wo)


def _ffn_kernel(h_ref, g_ref, w1_ref, w2_ref, gf_ref, o_ref, *, ff_chunk):
    h = h_ref[...]
    ms = jnp.mean(h * h, axis=-1, keepdims=True)
    xn = (h * lax.rsqrt(ms + NORM_EPS) * g_ref[...]).astype(bf16)
    acc = h
    for c in range(0, w1_ref.shape[1], ff_chunk):
        f = jnp.dot(xn, w1_ref[:, c:c + ff_chunk], preferred_element_type=f32)
        f = jnp.square(jnp.maximum(f, 0.0)).astype(bf16)
        acc = acc + jnp.dot(f, w2_ref[c:c + ff_chunk, :], preferred_element_type=f32)
    ms2 = jnp.mean(acc * acc, axis=-1, keepdims=True)
    o_ref[...] = acc * lax.rsqrt(ms2 + NORM_EPS) * gf_ref[...]


def _ffn(h2d, g, w1, w2, gf):
    n, d = h2d.shape
    tm = _pick_tile(n, (1024, ROW_TILE))
    row = pl.BlockSpec((tm, d), lambda i: (i, 0))
    return pl.pallas_call(
        functools.partial(_ffn_kernel, ff_chunk=512),
        grid=(n // tm,),
        in_specs=[row, _resident(g, 1), _resident(w1, 1), _resident(w2, 1), _resident(gf, 1)],
        out_specs=row,
        out_shape=jax.ShapeDtypeStruct((n, d), f32),
        compiler_params=pltpu.CompilerParams(
            dimension_semantics=("parallel",), vmem_limit_bytes=VMEM_LIMIT),
        name="ffn",
    )(h2d, g, w1, w2, gf)


def _rope_tables(n_tok, lk):
    rows = n_tok // GRID_W
    inv_freq = ROPE_THETA ** (-jnp.arange(AXIS_FREQS, dtype=f32) * 2.0 / AXIS_DIM)
    row_ang = jnp.arange(rows, dtype=f32)[:, None] * inv_freq
    col_ang = jnp.arange(GRID_W, dtype=f32)[:, None] * inv_freq
    grid = jnp.stack([jnp.broadcast_to(row_ang[:, None, :], (rows, GRID_W, AXIS_FREQS)),
                      jnp.broadcast_to(col_ang[None, :, :], (rows, GRID_W, AXIS_FREQS))], axis=2)
    grid = grid.reshape(rows * GRID_W, 2, AXIS_FREQS)
    ang = jnp.concatenate([jnp.zeros((lk - n_tok, 2, AXIS_FREQS), f32), grid], axis=0)
    cos, sin = jnp.cos(ang), jnp.sin(ang)
    cos64 = jnp.stack([cos, cos], axis=2).reshape(lk, HEAD_DIM)
    sin64 = jnp.stack([-sin, sin], axis=2).reshape(lk, HEAD_DIM)
    return jnp.tile(cos64, (1, 2)), jnp.tile(sin64, (1, 2))


def kernel(x, meta_tokens, mix_norm_g, w_in, rwkv_shift, decay_w0, decay_w2, icl_a0, icl_a2, gate_w2, k_k, k_a, r_k, lnx_g, lnx_b, q_norm_g, k_norm_g, w_branch_rwkv, w_branch_attn, w_out, ffn_norm_g, w_ff1, w_ff2, final_norm_g):
    b, n_tok, d = x.shape
    assert n_tok % ROW_TILE == 0 and d == COL_GATE_B
    lp = FRONT + n_tok
    lk = lp - KEY_ROW0

    w = w_in[0]
    s0 = RWKV_IN
    s1 = s0 + ATTN_Q_WIDTH
    s2 = s1 + ATTN_KV_WIDTH
    s3 = s2 + ATTN_KV_WIDTH
    s4 = s3 + d
    w_perm = jnp.concatenate(
        [w[:, s3:s4], w[:, s4:], w[:, :s0], jnp.zeros((d, RWKV_IN_PAD - RWKV_IN), w.dtype),
         w[:, s0:s1], w[:, s1:s2], w[:, s2:s3]], axis=1).astype(bf16)
    mu = jnp.pad(rwkv_shift[0], ((0, 0), (0, RWKV_IN_PAD - RWKV_IN)))
    zl = jnp.zeros((DECAY_LORA, RWKV_WIDTH), f32)
    w2cat = jnp.concatenate([jnp.concatenate([decay_w2[0, 0], zl], axis=1),
                             jnp.concatenate([zl, decay_w2[0, 1]], axis=1)], axis=0)
    a2cat = jnp.concatenate([jnp.concatenate([icl_a2[0, 0], zl], axis=1),
                             jnp.concatenate([zl, icl_a2[0, 1]], axis=1)], axis=0)
    w0cat = decay_w0[0].reshape(1, 2 * RWKV_WIDTH)
    a0cat = icl_a0[0].reshape(1, 2 * RWKV_WIDTH)
    g2pad = jnp.pad(gate_w2[0], ((0, GD_PAD - GATE_LORA), (0, 0)))
    row = lambda a: a.reshape(1, -1)

    meta_frame = jnp.concatenate([jnp.zeros((META_ROW0, d), x.dtype), meta_tokens.astype(x.dtype)])
    proj3 = _in_proj(meta_frame, x, row(mix_norm_g[0]), w_perm)

    cos, sin = _rope_tables(n_tok, lk)
    qg = jnp.tile(row(q_norm_g[0]), (1, 2))
    kg = jnp.tile(row(k_norm_g[0]), (1, 2))
    tok0 = FRONT - KEY_ROW0
    q = _q_prep(proj3, n_tok, cos[tok0:], sin[tok0:], qg)
    k4, v4 = _kv_prep(proj3, cos, sin, kg)
    att = _attention(q, k4, v4, n_tok)

    yf, yb, gate, bonus = _rwkv(proj3, n_tok, mu, w0cat, w2cat, a0cat, a2cat, g2pad,
                                row(k_k[0]), row(k_a[0]), row(r_k[0]))

    h1 = _post(yf, yb, gate, bonus, att, proj3, x, row(lnx_g[0]), row(lnx_b[0]),
               w_branch_rwkv[0].astype(bf16), w_branch_attn[0].astype(bf16),
               w_out[0].astype(bf16))
    out = _ffn(h1.reshape(b * n_tok, d), row(ffn_norm_g[0]), w_ff1[0].astype(bf16),
               w_ff2[0].astype(bf16), row(final_norm_g))
    return out.reshape(b, n_tok, d)
```

```python
import functools

import jax
import jax.numpy as jnp
from jax import lax
from jax.experimental import pallas as pl
from jax.experimental.pallas import tpu as pltpu

f32 = jnp.float32
bf16 = jnp.bfloat16

N_META = 16
GRID_W = 64
HEAD_DIM = 64
RWKV_HEADS = 8
RWKV_WIDTH = RWKV_HEADS * HEAD_DIM
DECAY_LORA = 64
ICL_LORA = 64
GATE_LORA = 160
LNX_EPS = 64e-5
ATTN_Q_HEADS = 8
ATTN_KV_HEADS = 2
ATTN_GROUP = ATTN_Q_HEADS // ATTN_KV_HEADS
ATTN_Q_WIDTH = ATTN_Q_HEADS * HEAD_DIM
ATTN_KV_WIDTH = ATTN_KV_HEADS * HEAD_DIM
ROPE_THETA = 10000.0
AXIS_DIM = HEAD_DIM // 2
AXIS_FREQS = AXIS_DIM // 2
NORM_EPS = 1e-6

LANES = 128
CHUNK = 64
HALO = 8
PAIR = 2 * HEAD_DIM
N_PAIRS = RWKV_WIDTH // PAIR
ROW_TILE = 512
FRONT = ROW_TILE
META_ROW0 = FRONT - N_META
SCAN_SUB = 2
SCAN_ROWS = SCAN_SUB * CHUNK
BLOCK0 = META_ROW0 // SCAN_ROWS
Q_TILE = 64
Q_SUB = 4
KEY_ROW0 = FRONT - LANES
LOG2E = 1.4426950408889634
MASK_BIAS = -1e30
VMEM_LIMIT = 56 * 1024 * 1024

RWKV_IN = 3 * RWKV_WIDTH + 2 * DECAY_LORA + 2 * ICL_LORA + GATE_LORA
RWKV_IN_PAD = 2048
COL_GATE_A = 0
COL_GATE_B = 1024
COL_RWKV = 2048
COL_Q = 4096
COL_K = 4608
COL_V = 4736
PROJ_W = 4864
OFF_R, OFF_K, OFF_V = 0, RWKV_WIDTH, 2 * RWKV_WIDTH
OFF_WD = 3 * RWKV_WIDTH
OFF_AD = OFF_WD + 2 * DECAY_LORA
OFF_GD = OFF_AD + 2 * ICL_LORA
GD_PAD = RWKV_IN_PAD - OFF_GD


def _pick_tile(n, candidates):
    for c in candidates:
        if n % c == 0:
            return c
    raise ValueError(f"no tile for {n} in {candidates}")


def _split3(x):
    h = x.astype(bf16)
    r = x - h.astype(f32)
    m = r.astype(bf16)
    l = (r - m.astype(f32)).astype(bf16)
    return h, m, l


def _dot(a, b, dims=None):
    if dims is None:
        return jnp.dot(a, b, preferred_element_type=f32)
    return lax.dot_general(a, b, (dims, ((), ())), preferred_element_type=f32)


def _mm(a, b):
    return _dot(a.astype(bf16), b.astype(bf16))


def _mm_exact_lhs(a_bf16, b):
    bh, bm, bl = _split3(b)
    return _dot(a_bf16, bh) + _dot(a_bf16, bm) + _dot(a_bf16, bl)


def _mm_exact_rhs(a, b_bf16):
    ah, am, al = _split3(a)
    return _dot(ah, b_bf16) + _dot(am, b_bf16) + _dot(al, b_bf16)


def _head_sum_matrix(scale):
    r = lax.broadcasted_iota(jnp.int32, (LANES, LANES), 0) // HEAD_DIM
    c = lax.broadcasted_iota(jnp.int32, (LANES, LANES), 1) // HEAD_DIM
    return jnp.where(r == c, scale, 0.0).astype(bf16)


def _head_sums(x, g):
    w = x.shape[-1]
    parts = [_mm_exact_rhs(x[:, i:i + LANES], g) for i in range(0, w, LANES)]
    return parts[0] if len(parts) == 1 else jnp.concatenate(parts, axis=-1)


def _resident(a, n_grid):
    zeros = (0,) * a.ndim
    return pl.BlockSpec(a.shape, lambda *_: zeros, pipeline_mode=pl.Buffered(1))


def _in_proj_kernel(mf_ref, x_ref, g_ref, w_ref, o_ref):
    first = pl.program_id(1) == 0
    half = ROW_TILE // 2
    us = []
    for r in (slice(0, half), slice(half, ROW_TILE)):
        x = jnp.where(first, mf_ref[r, :], x_ref[r, :])
        ms = jnp.mean(x * x, axis=-1, keepdims=True)
        us.append((x * lax.rsqrt(ms + NORM_EPS) * g_ref[...]).astype(bf16))
    for r, u in zip((slice(0, half), slice(half, ROW_TILE)), us):
        o_ref[r, :] = jnp.dot(u, w_ref[...], preferred_element_type=f32)


def _in_proj(meta_frame, x, g, w):
    b, n_tok, d = x.shape
    nx = n_tok // ROW_TILE
    return pl.pallas_call(
        _in_proj_kernel,
        grid=(b, nx + 1),
        in_specs=[_resident(meta_frame, 2),
                  pl.BlockSpec((None, ROW_TILE, d), lambda bi, i: (bi, jnp.maximum(i - 1, 0), 0)),
                  _resident(g, 2), _resident(w, 2)],
        out_specs=pl.BlockSpec((None, ROW_TILE, PROJ_W), lambda bi, i: (bi, i, 0)),
        out_shape=jax.ShapeDtypeStruct((b, FRONT + n_tok, PROJ_W), f32),
        compiler_params=pltpu.CompilerParams(
            dimension_semantics=("parallel", "arbitrary"), vmem_limit_bytes=VMEM_LIMIT),
        name="in_proj",
    )(meta_frame, x, g, w)


def _norm_rope(x, gain, cos, sin_signed, gsum, scale):
    ms = _mm_exact_rhs(x * x, gsum)
    xn = x * lax.rsqrt(ms + NORM_EPS) * gain
    lane = lax.broadcasted_iota(jnp.int32, x.shape, 1)
    first_half = (lane % AXIS_DIM) < AXIS_FREQS
    partner = jnp.where(first_half,
                        pltpu.roll(xn, LANES - AXIS_FREQS, axis=1),
                        pltpu.roll(xn, AXIS_FREQS, axis=1))
    out = xn * cos + partner * sin_signed
    return out * scale if scale != 1.0 else out


def _kv_kernel(k_ref, v_ref, cos_ref, sin_ref, kg_ref, ko_ref, vo_ref):
    gsum = _head_sum_matrix(1.0 / HEAD_DIM)
    kn = _norm_rope(k_ref[...], kg_ref[...], cos_ref[...], sin_ref[...], gsum, 1.0)
    vf = v_ref[...]
    t = kn.shape[0]
    lane = lax.broadcasted_iota(jnp.int32, kn.shape, 1)
    key_row = lax.broadcasted_iota(jnp.int32, kn.shape, 0) + pl.program_id(1) * t
    extra = lane == HEAD_DIM
    k_aug = jnp.where(extra & (key_row < META_ROW0 - KEY_ROW0), MASK_BIAS, 0.0)
    v_aug = jnp.where(extra, 1.0, 0.0)
    for h in range(ATTN_KV_HEADS):
        kh = kn if h == 0 else pltpu.roll(kn, HEAD_DIM, axis=1)
        vh = vf if h == 0 else pltpu.roll(vf, HEAD_DIM, axis=1)
        ko_ref[h] = jnp.where(lane < HEAD_DIM, kh, k_aug).astype(bf16)
        vo_ref[h] = jnp.where(lane < HEAD_DIM, vh, v_aug).astype(bf16)


def _kv_prep(proj3, cos, sin, kg):
    b, lp, _ = proj3.shape
    lk = lp - KEY_ROW0
    t = _pick_tile(lk, (KEY_ROW0, LANES))
    off = KEY_ROW0 // t
    row = lambda w, cb: pl.BlockSpec((None, t, w), lambda bi, ti: (bi, ti + off, cb))
    tab = pl.BlockSpec((t, LANES), lambda bi, ti: (ti, 0))
    vec = pl.BlockSpec((1, LANES), lambda bi, ti: (0, 0))
    kv_out = pl.BlockSpec((None, ATTN_KV_HEADS, t, LANES), lambda bi, ti: (bi, 0, ti, 0))
    kv_shape = jax.ShapeDtypeStruct((b, ATTN_KV_HEADS, lk, LANES), bf16)
    return pl.pallas_call(
        _kv_kernel,
        grid=(b, lk // t),
        in_specs=[row(ATTN_KV_WIDTH, COL_K // ATTN_KV_WIDTH),
                  row(ATTN_KV_WIDTH, COL_V // ATTN_KV_WIDTH),
                  tab, tab, vec],
        out_specs=[kv_out, kv_out],
        out_shape=[kv_shape, kv_shape],
        compiler_params=pltpu.CompilerParams(dimension_semantics=("parallel", "parallel")),
        name="kv_prep",
    )(proj3, proj3, cos, sin, kg)


def _attn_stages(q_ref, cos_ref, sin_ref, qg_ref, k_ref, v_ref, o_ref):
    tq = Q_TILE
    lane = lax.broadcasted_iota(jnp.int32, (tq, LANES), 1)
    one_hot = jnp.where(lane == HEAD_DIM, 1.0, 0.0)
    n_sub = q_ref.shape[0] // tq
    gsum = _head_sum_matrix(1.0 / HEAD_DIM)
    qn = [_norm_rope(q_ref[:, i:i + LANES], qg_ref[...], cos_ref[...], sin_ref[...], gsum,
                     HEAD_DIM ** -0.5 * LOG2E) for i in range(0, ATTN_GROUP * HEAD_DIM, LANES)]
    yield

    def scores(i):
        rows = []
        for g in range(ATTN_GROUP):
            blk = qn[g // 2][i * tq:(i + 1) * tq]
            if g % 2:
                blk = pltpu.roll(blk, HEAD_DIM, axis=1)
            rows.append(jnp.where(lane < HEAD_DIM, blk, one_hot))
        q = jnp.concatenate(rows, axis=0).astype(bf16)
        return lax.dot_general(q, k_ref[...], (((1,), (1,)), ((), ())), preferred_element_type=f32)

    def probs(s):
        return jnp.exp2(s - jnp.max(s, axis=-1, keepdims=True)).astype(bf16)

    def finish(i, p):
        ov = jnp.dot(p, v_ref[...], preferred_element_type=f32)
        on = ov / ov[:, HEAD_DIM:HEAD_DIM + 1]
        for g in range(0, ATTN_GROUP, 2):
            even = on[g * tq:(g + 1) * tq]
            odd = pltpu.roll(on[(g + 1) * tq:(g + 2) * tq], HEAD_DIM, axis=1)
            o_ref[i * tq:(i + 1) * tq, (g // 2) * LANES:(g // 2 + 1) * LANES] = jnp.where(
                lane < HEAD_DIM, even, odd)

    s, p = {}, {}
    for t in range(n_sub + 2):
        if t < n_sub:
            s[t] = scores(t)
        if 0 <= t - 1 < n_sub:
            p[t - 1] = probs(s.pop(t - 1))
        if 0 <= t - 2 < n_sub:
            finish(t - 2, p.pop(t - 2))
        yield


def _attn_specs(n_tok, lk):
    gw = ATTN_GROUP * HEAD_DIM
    tq = Q_TILE * Q_SUB
    nq = n_tok // tq
    q_row0, q_col0 = FRONT // tq, COL_Q // gw
    tile = lambda s: jnp.minimum(s, ATTN_KV_HEADS * nq - 1)
    head = lambda s: tile(s) // nq
    qi = lambda s: tile(s) % nq
    tab = pl.BlockSpec((tq, LANES), lambda bi, s: (qi(s), 0))
    kv = pl.BlockSpec((None, None, lk, LANES), lambda bi, s: (bi, head(s), 0, 0))
    ins = [pl.BlockSpec((None, tq, gw), lambda bi, s: (bi, qi(s) + q_row0, head(s) + q_col0)),
           tab, tab, pl.BlockSpec((1, LANES), lambda bi, s: (0, 0)), kv, kv]
    out = pl.BlockSpec((None, tq, gw), lambda bi, s: (bi, qi(s), head(s)))
    return ins, out


def _shifted_rows(p, prev_row, next_row):
    rows = lax.broadcasted_iota(jnp.int32, p.shape, 0)
    n = p.shape[0]
    prev = jnp.where(rows == 0, prev_row, pltpu.roll(p, 1, axis=0))
    nxt = jnp.where(rows == n - 1, next_row, pltpu.roll(p, n - 1, axis=0))
    return prev, nxt


def _stack_pair(x):
    lane = lax.broadcasted_iota(jnp.int32, x.shape, 1)
    first = lane < HEAD_DIM
    return jnp.concatenate([jnp.where(first, x, 0.0), jnp.where(first, 0.0, x)], axis=0)


STASH_NAMES = ("a", "b", "k", "v", "be", "ke")


def _chunk_terms(direction, p, prev_row, next_row, chunk_idx, prm, out):
    (mu_ref, w0_ref, w2_ref, a0_ref, a2_ref, g2_ref, kk_ref, ka_ref, rk_ref) = prm
    mu = mu_ref[...]
    prev, nxt = _shifted_rows(p, prev_row, next_row)
    z = p + mu[0:1] * (prev - p) + mu[1:2] * (nxt - p)

    rows = lax.broadcasted_iota(jnp.int32, (SCAN_ROWS, 1), 0) + chunk_idx * SCAN_ROWS
    valid = rows >= META_ROW0
    z = jnp.where(valid, z, 0.0)
    yield

    r = z[:, OFF_R:OFF_R + RWKV_WIDTH]
    k = z[:, OFF_K:OFF_K + RWKV_WIDTH]
    v = z[:, OFF_V:OFF_V + RWKV_WIDTH]
    lo, hi = direction * RWKV_WIDTH, (direction + 1) * RWKV_WIDTH

    x_w = w0_ref[...] + _mm(jnp.tanh(z[:, OFF_WD:OFF_AD]), w2_ref[...])
    logw = -jnp.exp(f32(-0.5)) * jax.nn.sigmoid(x_w[:, lo:hi])
    logw = jnp.where(valid, logw, 0.0)
    a_all = jax.nn.sigmoid(a0_ref[...] + _mm(z[:, OFF_AD:OFF_GD], a2_ref[...]))
    a_dir = a_all[:, lo:hi]
    yield

    gsum = _head_sum_matrix(1.0)
    kk = k * kk_ref[...]
    kk = kk / jnp.maximum(jnp.sqrt(_head_sums(kk * kk, gsum)), 1e-12)
    k_a = ka_ref[...]
    k_dir = k * (1.0 + (a_dir - 1.0) * k_a)
    kka = kk * a_dir
    yield

    if direction == 0:
        out["gate"] = _mm(jax.nn.sigmoid(z[:, OFF_GD:RWKV_IN_PAD]), g2_ref[...])
        a_mean = 0.5 * (a_all[:, :RWKV_WIDTH] + a_all[:, RWKV_WIDTH:])
        k_mean = k * (1.0 + (a_mean - 1.0) * k_a)
        out["bonus"] = _head_sums(r * k_mean * rk_ref[...], gsum) * v
        yield

    tr = lax.broadcasted_iota(jnp.int32, (SCAN_ROWS, SCAN_ROWS), 0)
    tc = lax.broadcasted_iota(jnp.int32, (SCAN_ROWS, SCAN_ROWS), 1)
    tri = ((tr // CHUNK) == (tc // CHUNK)) & ((tc <= tr) if direction == 0 else (tc >= tr))
    lc = _mm_exact_lhs(jnp.where(tri, 1.0, 0.0).astype(bf16), logw)
    last = CHUNK - 1 if direction == 0 else 0
    ltots = [lc[c * CHUNK + last:c * CHUNK + last + 1, :] for c in range(SCAN_SUB)]
    out["e_tot"] = [jnp.exp(t) for t in ltots]
    ltot = jnp.concatenate([jnp.broadcast_to(t, (CHUNK, RWKV_WIDTH)) for t in ltots], axis=0)
    yield

    sl = lambda j: slice(j * PAIR, (j + 1) * PAIR)
    stack = lambda x, dt: [[_stack_pair(x[c * CHUNK:(c + 1) * CHUNK, sl(j)]).astype(dt)
                            for j in range(N_PAIRS)] for c in range(SCAN_SUB)]
    e_neg = jnp.exp(-lc)
    out["b"] = stack(kka * e_neg, bf16)
    yield
    out["k"] = stack(k_dir * e_neg, bf16)
    yield
    e_end = jnp.exp(ltot - lc)
    out["be"] = stack(kka * e_end, bf16)
    yield
    out["ke"] = stack(k_dir * e_end, bf16)
    yield
    out["a"] = stack(-kk * jnp.exp(lc - logw), bf16)
    yield
    out["r"] = stack(r * jnp.exp(lc), f32)
    yield
    out["v"] = stack(v, bf16)


def _scan_units(units, ops, r_s, e_tot, h0, out):
    sr = lax.broadcasted_iota(jnp.int32, (PAIR, PAIR), 0)
    sc = lax.broadcasted_iota(jnp.int32, (PAIR, PAIR), 1)
    same = (sr // CHUNK) == (sc // CHUNK)
    tt, ss = sr % CHUNK, sc % CHUNK
    strict = (same & (ss < tt), same & (ss > tt))
    incl = (same & (ss <= tt), same & (ss >= tt))
    eye = sr == sc
    eye_f = jnp.where(eye, 1.0, 0.0)
    levels = []
    m = 2
    while m < CHUNK:
        levels.append(((sr // (2 * m)) == (sc // (2 * m))) & ((sr // m) != (sc // m)))
        m *= 2
    base = (sr // 2) == (sc // 2)

    nt = ((1,), (1,))
    tn = ((0,), (0,))
    cat0 = lambda u, w: jnp.concatenate([u, w], axis=0)
    cat1 = lambda u, w: jnp.concatenate([u, w], axis=1)
    sl = lambda j: slice(j * PAIR, (j + 1) * PAIR)
    a_s, b_s, k_s, v_s, be_s, ke_s = (ops[n] for n in STASH_NAMES)

    a_ab, a_ak, a_rb, a_rk = [], [], [], []
    for u, (d, _, _) in enumerate(units):
        scores = _dot(cat0(a_s[u], r_s[u].astype(bf16)), cat0(b_s[u], k_s[u]), nt)
        a_ab.append(jnp.where(strict[d], scores[:PAIR, :PAIR], 0.0))
        a_ak.append(jnp.where(strict[d], scores[:PAIR, PAIR:], 0.0).astype(bf16))
        a_rb.append(jnp.where(incl[d], scores[PAIR:, :PAIR], 0.0).astype(bf16))
        a_rk.append(jnp.where(incl[d], scores[PAIR:, PAIR:], 0.0).astype(bf16))
    yield

    t_inv = [eye_f + jnp.where(base, a, 0.0) for a in a_ab]

    akv = [_dot(cat0(x, y), v) for x, y, v in zip(a_ak, a_rk, v_s)]
    av = [x[:PAIR] for x in akv]
    y0b = [x[PAIR:] for x in akv]
    kv = [_dot(k, v, tn) for k, v in zip(ke_s, v_s)]
    yield
    for off in levels:
        tb = [t.astype(bf16) for t in t_inv]
        at = [_dot(jnp.where(off, a, 0.0).astype(bf16), t) for a, t in zip(a_ab, tb)]
        yield
        t_inv = [t + _dot(t16, x.astype(bf16)) for t, t16, x in zip(t_inv, tb, at)]
        yield

    w2 = [_dot(t.astype(bf16), cat1(a, x.astype(bf16))).astype(bf16)
          for t, a, x in zip(t_inv, a_s, av)]
    yield
    qy = [_dot(a, w) for a, w in zip(a_rb, w2)]
    mn = [_dot(b, w, tn) for b, w in zip(be_s, w2)]
    yield
    qh = [cat0(r + x[:, :PAIR], y[:, :PAIR]).astype(bf16) for r, x, y in zip(r_s, qy, mn)]
    yield
    out["y"], out["h"] = {}, dict(h0)
    for pos in range(SCAN_SUB):
        for u, (d, c, j) in enumerate(units):
            if c != (pos if d == 0 else SCAN_SUB - 1 - pos):
                continue
            h = out["h"][d, j]
            qm = _dot(qh[u], h.astype(bf16))
            y_st = qm[:PAIR] + qy[u][:, PAIR:] + y0b[u]
            out["y"][d, c, j] = y_st[:CHUNK] + y_st[CHUNK:]
            decay_col = jnp.sum(jnp.where(eye, e_tot[d][c][:, sl(j)], 0.0), axis=1, keepdims=True)
            out["h"][d, j] = decay_col * h + qm[PAIR:] + mn[u][:, PAIR:] + kv[u]
        yield


def _mixer_kernel(pf_ref, pf_prev_ref, pf_next_ref, pb_ref, pb_prev_ref, pb_next_ref,
                  mu_ref, w0_ref, w2_ref, a0_ref, a2_ref, g2_ref, kk_ref, ka_ref, rk_ref,
                  q_ref, cos_ref, sin_ref, qg_ref, k_ref, v_ref,
                  yf_ref, yb_ref, gate_ref, bonus_ref, att_ref,
                  h_ref, ops_ref, r_ref, e_ref, *, n_blocks):
    s = pl.program_id(1)

    @pl.when(s == 0)
    def _():
        ops_ref[...] = jnp.zeros_like(ops_ref)
        r_ref[...] = jnp.zeros_like(r_ref)
        e_ref[...] = jnp.zeros_like(e_ref)

    @pl.when(s <= 1)
    def _():
        h_ref[...] = jnp.zeros_like(h_ref)

    units = [(d, c, j) for d in range(2) for c in range(SCAN_SUB) for j in range(N_PAIRS)]
    sl = lambda j: slice(j * PAIR, (j + 1) * PAIR)
    rows = lambda c: slice(c * CHUNK, (c + 1) * CHUNK)

    ops = {n: [ops_ref[d, i, c, j] for d, c, j in units] for i, n in enumerate(STASH_NAMES)}
    e_tot = [[e_ref[d, c] for c in range(SCAN_SUB)] for d in range(2)]
    h0 = {(d, j): h_ref[d, j] for d in range(2) for j in range(N_PAIRS)}
    scan = {}
    scan_gen = _scan_units(units, ops, [r_ref[d, c, j] for d, c, j in units], e_tot, h0, scan)

    prm = (mu_ref, w0_ref, w2_ref, a0_ref, a2_ref, g2_ref, kk_ref, ka_ref, rk_ref)
    terms = [{}, {}]
    stages = [scan_gen, _attn_stages(q_ref, cos_ref, sin_ref, qg_ref, k_ref, v_ref, att_ref)]
    for direction, (p_ref, prev_ref, next_ref) in enumerate(
            ((pf_ref, pf_prev_ref, pf_next_ref), (pb_ref, pb_prev_ref, pb_next_ref))):
        bi = jnp.minimum(s, n_blocks - 1) if direction == 0 else jnp.maximum(n_blocks - 1 - s, 0)
        prev_row = jnp.where(bi == 0, 0.0, prev_ref[HALO - 1:HALO, :])
        next_row = jnp.where(bi == n_blocks - 1, 0.0, next_ref[0:1, :])
        stages.append(_chunk_terms(direction, p_ref[...], prev_row, next_row, bi + BLOCK0, prm,
                                   terms[direction]))
    while stages:
        for g in list(stages):
            if next(g, stages) is stages:
                stages.remove(g)

    for (d, j), h in scan["h"].items():
        h_ref[d, j] = h
    for d, c, j in units:
        for i, n in enumerate(STASH_NAMES):
            ops_ref[d, i, c, j] = terms[d][n][c][j]
        r_ref[d, c, j] = terms[d]["r"][c][j]
    for d in range(2):
        for c in range(SCAN_SUB):
            e_ref[d, c] = terms[d]["e_tot"][c]

    @pl.when((s >= 1) & (s <= n_blocks - 1))
    def _():
        gate_ref[...] = terms[0]["gate"]
        bonus_ref[...] = terms[0]["bonus"]

    @pl.when(s >= 2)
    def _():
        for d, c, j in units:
            if d == 0:
                yf_ref[rows(c), sl(j)] = scan["y"][d, c, j]

    @pl.when((s >= 1) & (s <= n_blocks - 1))
    def _():
        for d, c, j in units:
            if d == 1:
                yb_ref[rows(c), sl(j)] = scan["y"][d, c, j]


def _mixers(proj3, n_tok, mu, w0, w2, a0, a2, g2, k_k, k_a, r_k, cos, sin, qg, k4, v4):
    b, lp, _ = proj3.shape
    nb = lp // SCAN_ROWS - BLOCK0
    hb = SCAN_ROWS // HALO
    cb = COL_RWKV // RWKV_IN_PAD
    h_last = (lp // SCAN_ROWS) * hb - 1
    block_of = (lambda s: jnp.minimum(s, nb - 1) + BLOCK0,
                lambda s: jnp.maximum(nb - 1 - s, 0) + BLOCK0)
    main = lambda d: pl.BlockSpec((None, SCAN_ROWS, RWKV_IN_PAD),
                                  lambda bi, s: (bi, block_of[d](s), cb))
    prev = lambda d: pl.BlockSpec((None, HALO, RWKV_IN_PAD),
                                  lambda bi, s: (bi, block_of[d](s) * hb - 1, cb))
    nxt = lambda d: pl.BlockSpec(
        (None, HALO, RWKV_IN_PAD),
        lambda bi, s: (bi, jnp.minimum((block_of[d](s) + 1) * hb, h_last), cb))
    blk = lambda f: pl.BlockSpec((None, SCAN_ROWS, RWKV_WIDTH), lambda bi, s: (bi, f(s), 0))
    y_fwd = blk(lambda s: jnp.maximum(s - 2, 0))
    y_bwd = blk(lambda s: jnp.clip(nb - s - 1, 0, nb - 2))
    prep = blk(lambda s: jnp.clip(s - 1, 0, nb - 2))
    shp = jax.ShapeDtypeStruct((b, n_tok, RWKV_WIDTH), f32)
    params = (mu, w0, w2, a0, a2, g2, k_k, k_a, r_k)
    n_steps = nb + 1
    assert n_steps >= ATTN_KV_HEADS * n_tok // (Q_TILE * Q_SUB)
    attn_in, attn_out = _attn_specs(n_tok, k4.shape[2])
    return pl.pallas_call(
        functools.partial(_mixer_kernel, n_blocks=nb),
        grid=(b, n_steps),
        in_specs=[main(0), prev(0), nxt(0), main(1), prev(1), nxt(1)]
                 + [_resident(a, 2) for a in params] + attn_in,
        out_specs=[y_fwd, y_bwd, prep, prep, attn_out],
        out_shape=[shp, shp, shp, shp, jax.ShapeDtypeStruct((b, n_tok, ATTN_Q_WIDTH), f32)],
        scratch_shapes=[pltpu.VMEM((2, N_PAIRS, PAIR, PAIR), f32),
                        pltpu.VMEM((2, len(STASH_NAMES), SCAN_SUB, N_PAIRS, PAIR, PAIR), bf16),
                        pltpu.VMEM((2, SCAN_SUB, N_PAIRS, PAIR, PAIR), f32),
                        pltpu.VMEM((2, SCAN_SUB, 1, RWKV_WIDTH), f32)],
        compiler_params=pltpu.CompilerParams(
            dimension_semantics=("parallel", "arbitrary"), vmem_limit_bytes=VMEM_LIMIT),
        name="mixers",
    )(proj3, proj3, proj3, proj3, proj3, proj3, *params, proj3, cos, sin, qg, k4, v4)


def _post_kernel(yf_ref, yb_ref, gate_ref, bonus_ref, att_ref, ga_ref, gb_ref, h_ref,
                 lng_ref, lnb_ref, wa_ref, wb_ref, wo_ref, o_ref):
    gmean = _head_sum_matrix(1.0 / HEAD_DIM)
    y = yf_ref[...] + yb_ref[...]
    mu = _head_sums(y, gmean)
    d = y - mu
    var = _head_sums(d * d, gmean)
    yn = d * lax.rsqrt(var + LNX_EPS) * lng_ref[...] + lnb_ref[...]
    out_a = ((yn + bonus_ref[...]) * gate_ref[...]).astype(bf16)
    ya = jnp.dot(out_a, wa_ref[...], preferred_element_type=f32)
    yb = jnp.dot(att_ref[...].astype(bf16), wb_ref[...], preferred_element_type=f32)
    merged = jax.nn.sigmoid(ga_ref[...]) * ya + jax.nn.sigmoid(gb_ref[...]) * yb
    o_ref[...] = h_ref[...] + jnp.dot(merged.astype(bf16), wo_ref[...], preferred_element_type=f32)


def _post(yf, yb, gate, bonus, att, proj3, x, lng, lnb, wa, wb, wo):
    b, n_tok, d = x.shape
    tm = ROW_TILE
    off = FRONT // tm
    row = lambda w: pl.BlockSpec((None, tm, w), lambda bi, i: (bi, i, 0))
    gate_cols = lambda cb: pl.BlockSpec((None, tm, d), lambda bi, i: (bi, i + off, cb))
    return pl.pallas_call(
        _post_kernel,
        grid=(b, n_tok // tm),
        in_specs=[row(RWKV_WIDTH), row(RWKV_WIDTH), row(RWKV_WIDTH), row(RWKV_WIDTH),
                  row(ATTN_Q_WIDTH), gate_cols(COL_GATE_A // d), gate_cols(COL_GATE_B // d), row(d),
                  _resident(lng, 2), _resident(lnb, 2), _resident(wa, 2), _resident(wb, 2),
                  _resident(wo, 2)],
        out_specs=row(d),
        out_shape=jax.ShapeDtypeStruct((b, n_tok, d), f32),
        compiler_params=pltpu.CompilerParams(
            dimension_semantics=("parallel", "parallel"), vmem_limit_bytes=VMEM_LIMIT),
        name="post",
    )(yf, yb, gate, bonus, att, proj3, proj3, x, lng, lnb, wa, wb, wo)


def _ffn_kernel(h_ref, g_ref, w1_ref, w2_ref, gf_ref, o_ref, *, ff_chunk):
    h = h_ref[...]
    ms = jnp.mean(h * h, axis=-1, keepdims=True)
    xn = (h * lax.rsqrt(ms + NORM_EPS) * g_ref[...]).astype(bf16)
    acc = h
    for c in range(0, w1_ref.shape[1], ff_chunk):
        f = jnp.dot(xn, w1_ref[:, c:c + ff_chunk], preferred_element_type=f32)
        f = jnp.square(jnp.maximum(f, 0.0)).astype(bf16)
        acc = acc + jnp.dot(f, w2_ref[c:c + ff_chunk, :], preferred_element_type=f32)
    ms2 = jnp.mean(acc * acc, axis=-1, keepdims=True)
    o_ref[...] = acc * lax.rsqrt(ms2 + NORM_EPS) * gf_ref[...]


def _ffn(h2d, g, w1, w2, gf):
    n, d = h2d.shape
    tm = _pick_tile(n, (1024, ROW_TILE))
    row = pl.BlockSpec((tm, d), lambda i: (i, 0))
    return pl.pallas_call(
        functools.partial(_ffn_kernel, ff_chunk=512),
        grid=(n // tm,),
        in_specs=[row, _resident(g, 1), _resident(w1, 1), _resident(w2, 1), _resident(gf, 1)],
        out_specs=row,
        out_shape=jax.ShapeDtypeStruct((n, d), f32),
        compiler_params=pltpu.CompilerParams(
            dimension_semantics=("parallel",), vmem_limit_bytes=VMEM_LIMIT),
        name="ffn",
    )(h2d, g, w1, w2, gf)


def _rope_tables(n_tok, lk):
    rows = n_tok // GRID_W
    inv_freq = ROPE_THETA ** (-jnp.arange(AXIS_FREQS, dtype=f32) * 2.0 / AXIS_DIM)
    row_ang = jnp.arange(rows, dtype=f32)[:, None] * inv_freq
    col_ang = jnp.arange(GRID_W, dtype=f32)[:, None] * inv_freq
    grid = jnp.stack([jnp.broadcast_to(row_ang[:, None, :], (rows, GRID_W, AXIS_FREQS)),
                      jnp.broadcast_to(col_ang[None, :, :], (rows, GRID_W, AXIS_FREQS))], axis=2)
    grid = grid.reshape(rows * GRID_W, 2, AXIS_FREQS)
    ang = jnp.concatenate([jnp.zeros((lk - n_tok, 2, AXIS_FREQS), f32), grid], axis=0)
    cos, sin = jnp.cos(ang), jnp.sin(ang)
    cos64 = jnp.stack([cos, cos], axis=2).reshape(lk, HEAD_DIM)
    sin64 = jnp.stack([-sin, sin], axis=2).reshape(lk, HEAD_DIM)
    return jnp.tile(cos64, (1, 2)), jnp.tile(sin64, (1, 2))


def kernel(x, meta_tokens, mix_norm_g, w_in, rwkv_shift, decay_w0, decay_w2, icl_a0, icl_a2, gate_w2, k_k, k_a, r_k, lnx_g, lnx_b, q_norm_g, k_norm_g, w_branch_rwkv, w_branch_attn, w_out, ffn_norm_g, w_ff1, w_ff2, final_norm_g):
    b, n_tok, d = x.shape
    assert n_tok % ROW_TILE == 0 and d == COL_GATE_B
    lp = FRONT + n_tok
    lk = lp - KEY_ROW0

    w = w_in[0]
    s0 = RWKV_IN
    s1 = s0 + ATTN_Q_WIDTH
    s2 = s1 + ATTN_KV_WIDTH
    s3 = s2 + ATTN_KV_WIDTH
    s4 = s3 + d
    w_perm = jnp.concatenate(
        [w[:, s3:s4], w[:, s4:], w[:, :s0], jnp.zeros((d, RWKV_IN_PAD - RWKV_IN), w.dtype),
         w[:, s0:s1], w[:, s1:s2], w[:, s2:s3]], axis=1).astype(bf16)
    mu = jnp.pad(rwkv_shift[0], ((0, 0), (0, RWKV_IN_PAD - RWKV_IN)))
    zl = jnp.zeros((DECAY_LORA, RWKV_WIDTH), f32)
    w2cat = jnp.concatenate([jnp.concatenate([decay_w2[0, 0], zl], axis=1),
                             jnp.concatenate([zl, decay_w2[0, 1]], axis=1)], axis=0)
    a2cat = jnp.concatenate([jnp.concatenate([icl_a2[0, 0], zl], axis=1),
                             jnp.concatenate([zl, icl_a2[0, 1]], axis=1)], axis=0)
    w0cat = decay_w0[0].reshape(1, 2 * RWKV_WIDTH)
    a0cat = icl_a0[0].reshape(1, 2 * RWKV_WIDTH)
    g2pad = jnp.pad(gate_w2[0], ((0, GD_PAD - GATE_LORA), (0, 0)))
    row = lambda a: a.reshape(1, -1)

    meta_frame = jnp.concatenate([jnp.zeros((META_ROW0, d), x.dtype), meta_tokens.astype(x.dtype)])
    proj3 = _in_proj(meta_frame, x, row(mix_norm_g[0]), w_perm)

    cos, sin = _rope_tables(n_tok, lk)
    qg = jnp.tile(row(q_norm_g[0]), (1, 2))
    kg = jnp.tile(row(k_norm_g[0]), (1, 2))
    tok0 = FRONT - KEY_ROW0
    k4, v4 = _kv_prep(proj3, cos, sin, kg)
    yf, yb, gate, bonus, att = _mixers(proj3, n_tok, mu, w0cat, w2cat, a0cat, a2cat, g2pad,
                                       row(k_k[0]), row(k_a[0]), row(r_k[0]),
                                       cos[tok0:], sin[tok0:], qg, k4, v4)

    h1 = _post(yf, yb, gate, bonus, att, proj3, x, row(lnx_g[0]), row(lnx_b[0]),
               w_branch_rwkv[0].astype(bf16), w_branch_attn[0].astype(bf16),
               w_out[0].astype(bf16))
    out = _ffn(h1.reshape(b * n_tok, d), row(ffn_norm_g[0]), w_ff1[0].astype(bf16),
               w_ff2[0].astype(bf16), row(final_norm_g))
    return out.reshape(b, n_tok, d)
```

```python
import functools

import jax
import jax.numpy as jnp
from jax import lax
from jax.experimental import pallas as pl
from jax.experimental.pallas import tpu as pltpu

f32 = jnp.float32
bf16 = jnp.bfloat16

N_META = 16
GRID_W = 64
HEAD_DIM = 64
RWKV_HEADS = 8
RWKV_WIDTH = RWKV_HEADS * HEAD_DIM
DECAY_LORA = 64
ICL_LORA = 64
GATE_LORA = 160
LNX_EPS = 64e-5
ATTN_Q_HEADS = 8
ATTN_KV_HEADS = 2
ATTN_GROUP = ATTN_Q_HEADS // ATTN_KV_HEADS
ATTN_Q_WIDTH = ATTN_Q_HEADS * HEAD_DIM
ATTN_KV_WIDTH = ATTN_KV_HEADS * HEAD_DIM
ROPE_THETA = 10000.0
AXIS_DIM = HEAD_DIM // 2
AXIS_FREQS = AXIS_DIM // 2
NORM_EPS = 1e-6

LANES = 128
CHUNK = 64
HALO = 8
PAIR = 2 * HEAD_DIM
N_PAIRS = RWKV_WIDTH // PAIR
ROW_TILE = 512
FRONT = ROW_TILE
META_ROW0 = FRONT - N_META
SCAN_SUB = 2
SCAN_ROWS = SCAN_SUB * CHUNK
BLOCK0 = META_ROW0 // SCAN_ROWS
Q_TILE = 64
Q_SUB = 4
KEY_ROW0 = FRONT - LANES
LOG2E = 1.4426950408889634
MASK_BIAS = -1e30
VMEM_LIMIT = 56 * 1024 * 1024

RWKV_IN = 3 * RWKV_WIDTH + 2 * DECAY_LORA + 2 * ICL_LORA + GATE_LORA
RWKV_IN_PAD = 2048
COL_GATE_A = 0
COL_GATE_B = 1024
COL_RWKV = 2048
COL_Q = 4096
COL_K = 4608
COL_V = 4736
PROJ_W = 4864
OFF_R, OFF_K, OFF_V = 0, RWKV_WIDTH, 2 * RWKV_WIDTH
OFF_WD = 3 * RWKV_WIDTH
OFF_AD = OFF_WD + 2 * DECAY_LORA
OFF_GD = OFF_AD + 2 * ICL_LORA
GD_PAD = RWKV_IN_PAD - OFF_GD


def _pick_tile(n, candidates):
    for c in candidates:
        if n % c == 0:
            return c
    raise ValueError(f"no tile for {n} in {candidates}")


def _split3(x):
    h = x.astype(bf16)
    r = x - h.astype(f32)
    m = r.astype(bf16)
    l = (r - m.astype(f32)).astype(bf16)
    return h, m, l


def _dot(a, b, dims=None):
    if dims is None:
        return jnp.dot(a, b, preferred_element_type=f32)
    return lax.dot_general(a, b, (dims, ((), ())), preferred_element_type=f32)


def _mm(a, b):
    return _dot(a.astype(bf16), b.astype(bf16))


def _mm_exact_lhs(a_bf16, b):
    bh, bm, bl = _split3(b)
    return _dot(a_bf16, bh) + _dot(a_bf16, bm) + _dot(a_bf16, bl)


def _mm_exact_rhs(a, b_bf16):
    ah, am, al = _split3(a)
    return _dot(ah, b_bf16) + _dot(am, b_bf16) + _dot(al, b_bf16)


def _head_sum_matrix(scale):
    r = lax.broadcasted_iota(jnp.int32, (LANES, LANES), 0) // HEAD_DIM
    c = lax.broadcasted_iota(jnp.int32, (LANES, LANES), 1) // HEAD_DIM
    return jnp.where(r == c, scale, 0.0).astype(bf16)


def _head_sums(x, g):
    w = x.shape[-1]
    parts = [_mm_exact_rhs(x[:, i:i + LANES], g) for i in range(0, w, LANES)]
    return parts[0] if len(parts) == 1 else jnp.concatenate(parts, axis=-1)


def _resident(a, n_grid):
    zeros = (0,) * a.ndim
    return pl.BlockSpec(a.shape, lambda *_: zeros, pipeline_mode=pl.Buffered(1))


def _in_proj_kernel(mf_ref, x_ref, g_ref, w_ref, o_ref):
    first = pl.program_id(1) == 0
    half = ROW_TILE // 2
    us = []
    for r in (slice(0, half), slice(half, ROW_TILE)):
        x = jnp.where(first, mf_ref[r, :], x_ref[r, :])
        ms = jnp.mean(x * x, axis=-1, keepdims=True)
        us.append((x * lax.rsqrt(ms + NORM_EPS) * g_ref[...]).astype(bf16))
    for r, u in zip((slice(0, half), slice(half, ROW_TILE)), us):
        o_ref[r, :] = jnp.dot(u, w_ref[...], preferred_element_type=f32)


def _in_proj(meta_frame, x, g, w):
    b, n_tok, d = x.shape
    nx = n_tok // ROW_TILE
    return pl.pallas_call(
        _in_proj_kernel,
        grid=(b, nx + 1),
        in_specs=[_resident(meta_frame, 2),
                  pl.BlockSpec((None, ROW_TILE, d), lambda bi, i: (bi, jnp.maximum(i - 1, 0), 0)),
                  _resident(g, 2), _resident(w, 2)],
        out_specs=pl.BlockSpec((None, ROW_TILE, PROJ_W), lambda bi, i: (bi, i, 0)),
        out_shape=jax.ShapeDtypeStruct((b, FRONT + n_tok, PROJ_W), f32),
        compiler_params=pltpu.CompilerParams(
            dimension_semantics=("parallel", "arbitrary"), vmem_limit_bytes=VMEM_LIMIT),
        name="in_proj",
    )(meta_frame, x, g, w)


def _norm_rope(x, gain, cos, sin_signed, gsum, scale):
    ms = _mm_exact_rhs(x * x, gsum)
    xn = x * lax.rsqrt(ms + NORM_EPS) * gain
    lane = lax.broadcasted_iota(jnp.int32, x.shape, 1)
    first_half = (lane % AXIS_DIM) < AXIS_FREQS
    partner = jnp.where(first_half,
                        pltpu.roll(xn, LANES - AXIS_FREQS, axis=1),
                        pltpu.roll(xn, AXIS_FREQS, axis=1))
    out = xn * cos + partner * sin_signed
    return out * scale if scale != 1.0 else out


def _kv_kernel(k_ref, v_ref, cos_ref, sin_ref, kg_ref, ko_ref, vo_ref):
    gsum = _head_sum_matrix(1.0 / HEAD_DIM)
    kn = _norm_rope(k_ref[...], kg_ref[...], cos_ref[...], sin_ref[...], gsum, 1.0)
    vf = v_ref[...]
    t = kn.shape[0]
    lane = lax.broadcasted_iota(jnp.int32, kn.shape, 1)
    key_row = lax.broadcasted_iota(jnp.int32, kn.shape, 0) + pl.program_id(1) * t
    extra = lane == HEAD_DIM
    k_aug = jnp.where(extra & (key_row < META_ROW0 - KEY_ROW0), MASK_BIAS, 0.0)
    v_aug = jnp.where(extra, 1.0, 0.0)
    for h in range(ATTN_KV_HEADS):
        kh = kn if h == 0 else pltpu.roll(kn, HEAD_DIM, axis=1)
        vh = vf if h == 0 else pltpu.roll(vf, HEAD_DIM, axis=1)
        ko_ref[h] = jnp.where(lane < HEAD_DIM, kh, k_aug).astype(bf16)
        vo_ref[h] = jnp.where(lane < HEAD_DIM, vh, v_aug).astype(bf16)


def _kv_prep(proj3, cos, sin, kg):
    b, lp, _ = proj3.shape
    lk = lp - KEY_ROW0
    t = _pick_tile(lk, (KEY_ROW0, LANES))
    off = KEY_ROW0 // t
    row = lambda w, cb: pl.BlockSpec((None, t, w), lambda bi, ti: (bi, ti + off, cb))
    tab = pl.BlockSpec((t, LANES), lambda bi, ti: (ti, 0))
    vec = pl.BlockSpec((1, LANES), lambda bi, ti: (0, 0))
    kv_out = pl.BlockSpec((None, ATTN_KV_HEADS, t, LANES), lambda bi, ti: (bi, 0, ti, 0))
    kv_shape = jax.ShapeDtypeStruct((b, ATTN_KV_HEADS, lk, LANES), bf16)
    return pl.pallas_call(
        _kv_kernel,
        grid=(b, lk // t),
        in_specs=[row(ATTN_KV_WIDTH, COL_K // ATTN_KV_WIDTH),
                  row(ATTN_KV_WIDTH, COL_V // ATTN_KV_WIDTH),
                  tab, tab, vec],
        out_specs=[kv_out, kv_out],
        out_shape=[kv_shape, kv_shape],
        compiler_params=pltpu.CompilerParams(dimension_semantics=("parallel", "parallel")),
        name="kv_prep",
    )(proj3, proj3, cos, sin, kg)


def _attn_stages(q_ref, cos_ref, sin_ref, qg_ref, k_ref, v_ref, o_ref):
    tq = Q_TILE
    lane = lax.broadcasted_iota(jnp.int32, (tq, LANES), 1)
    one_hot = jnp.where(lane == HEAD_DIM, 1.0, 0.0)
    n_sub = q_ref.shape[0] // tq
    gsum = _head_sum_matrix(1.0 / HEAD_DIM)
    qn = [_norm_rope(q_ref[:, i:i + LANES], qg_ref[...], cos_ref[...], sin_ref[...], gsum,
                     HEAD_DIM ** -0.5 * LOG2E) for i in range(0, ATTN_GROUP * HEAD_DIM, LANES)]
    yield

    def scores(i):
        rows = []
        for g in range(ATTN_GROUP):
            blk = qn[g // 2][i * tq:(i + 1) * tq]
            if g % 2:
                blk = pltpu.roll(blk, HEAD_DIM, axis=1)
            rows.append(jnp.where(lane < HEAD_DIM, blk, one_hot))
        q = jnp.concatenate(rows, axis=0).astype(bf16)
        return lax.dot_general(q, k_ref[...], (((1,), (1,)), ((), ())), preferred_element_type=f32)

    def probs(s):
        return jnp.exp2(s - jnp.max(s, axis=-1, keepdims=True)).astype(bf16)

    def finish(i, p):
        ov = jnp.dot(p, v_ref[...], preferred_element_type=f32)
        on = ov / ov[:, HEAD_DIM:HEAD_DIM + 1]
        for g in range(0, ATTN_GROUP, 2):
            even = on[g * tq:(g + 1) * tq]
            odd = pltpu.roll(on[(g + 1) * tq:(g + 2) * tq], HEAD_DIM, axis=1)
            o_ref[i * tq:(i + 1) * tq, (g // 2) * LANES:(g // 2 + 1) * LANES] = jnp.where(
                lane < HEAD_DIM, even, odd).astype(o_ref.dtype)

    s, p = {}, {}
    for t in range(n_sub + 2):
        if t < n_sub:
            s[t] = scores(t)
        if 0 <= t - 1 < n_sub:
            p[t - 1] = probs(s.pop(t - 1))
        if 0 <= t - 2 < n_sub:
            finish(t - 2, p.pop(t - 2))
        yield


def _attn_specs(n_tok, lk):
    gw = ATTN_GROUP * HEAD_DIM
    tq = Q_TILE * Q_SUB
    nq = n_tok // tq
    q_row0, q_col0 = FRONT // tq, COL_Q // gw
    tile = lambda s: jnp.minimum(s, ATTN_KV_HEADS * nq - 1)
    head = lambda s: tile(s) // nq
    qi = lambda s: tile(s) % nq
    tab = pl.BlockSpec((tq, LANES), lambda bi, s: (qi(s), 0))
    kv = pl.BlockSpec((None, None, lk, LANES), lambda bi, s: (bi, head(s), 0, 0))
    ins = [pl.BlockSpec((None, tq, gw), lambda bi, s: (bi, qi(s) + q_row0, head(s) + q_col0)),
           tab, tab, pl.BlockSpec((1, LANES), lambda bi, s: (0, 0)), kv, kv]
    out = pl.BlockSpec((None, tq, gw), lambda bi, s: (bi, qi(s), head(s)))
    return ins, out


def _shifted_rows(p, prev_row, next_row):
    rows = lax.broadcasted_iota(jnp.int32, p.shape, 0)
    n = p.shape[0]
    prev = jnp.where(rows == 0, prev_row, pltpu.roll(p, 1, axis=0))
    nxt = jnp.where(rows == n - 1, next_row, pltpu.roll(p, n - 1, axis=0))
    return prev, nxt


def _stack_pair(x):
    lane = lax.broadcasted_iota(jnp.int32, x.shape, 1)
    first = lane < HEAD_DIM
    return jnp.concatenate([jnp.where(first, x, 0.0), jnp.where(first, 0.0, x)], axis=0)


STASH_NAMES = ("a", "b", "k", "v", "be", "ke")


class _LazyLoads:
    def __init__(self, load, n):
        self._load, self._n = load, n

    def __len__(self):
        return self._n

    def __getitem__(self, u):
        return self._load(u)

    def __iter__(self):
        return (self._load(u) for u in range(self._n))


def _chunk_terms(direction, p, prev_row, next_row, chunk_idx, prm, out):
    (mu_ref, w0_ref, w2_ref, a0_ref, a2_ref, g2_ref, kk_ref, ka_ref, rk_ref) = prm
    mu = mu_ref[...]
    prev, nxt = _shifted_rows(p, prev_row, next_row)
    z = p + mu[0:1] * (prev - p) + mu[1:2] * (nxt - p)

    rows = lax.broadcasted_iota(jnp.int32, (SCAN_ROWS, 1), 0) + chunk_idx * SCAN_ROWS
    valid = rows >= META_ROW0
    z = jnp.where(valid, z, 0.0)
    yield

    r = z[:, OFF_R:OFF_R + RWKV_WIDTH]
    k = z[:, OFF_K:OFF_K + RWKV_WIDTH]
    v = z[:, OFF_V:OFF_V + RWKV_WIDTH]
    lo, hi = direction * RWKV_WIDTH, (direction + 1) * RWKV_WIDTH

    x_w = w0_ref[...] + _mm(jnp.tanh(z[:, OFF_WD:OFF_AD]), w2_ref[...])
    logw = -jnp.exp(f32(-0.5)) * jax.nn.sigmoid(x_w[:, lo:hi])
    logw = jnp.where(valid, logw, 0.0)
    a_all = jax.nn.sigmoid(a0_ref[...] + _mm(z[:, OFF_AD:OFF_GD], a2_ref[...]))
    a_dir = a_all[:, lo:hi]
    yield

    gsum = _head_sum_matrix(1.0)
    kk = k * kk_ref[...]
    kk = kk / jnp.maximum(jnp.sqrt(_head_sums(kk * kk, gsum)), 1e-12)
    k_a = ka_ref[...]
    k_dir = k * (1.0 + (a_dir - 1.0) * k_a)
    kka = kk * a_dir
    yield

    if direction == 0:
        out["gate"] = _mm(jax.nn.sigmoid(z[:, OFF_GD:RWKV_IN_PAD]), g2_ref[...])
        a_mean = 0.5 * (a_all[:, :RWKV_WIDTH] + a_all[:, RWKV_WIDTH:])
        k_mean = k * (1.0 + (a_mean - 1.0) * k_a)
        out["bonus"] = _head_sums(r * k_mean * rk_ref[...], gsum) * v
        yield

    tr = lax.broadcasted_iota(jnp.int32, (SCAN_ROWS, SCAN_ROWS), 0)
    tc = lax.broadcasted_iota(jnp.int32, (SCAN_ROWS, SCAN_ROWS), 1)
    tri = ((tr // CHUNK) == (tc // CHUNK)) & ((tc <= tr) if direction == 0 else (tc >= tr))
    lc = _mm_exact_lhs(jnp.where(tri, 1.0, 0.0).astype(bf16), logw)
    last = CHUNK - 1 if direction == 0 else 0
    ltots = [lc[c * CHUNK + last:c * CHUNK + last + 1, :] for c in range(SCAN_SUB)]
    out["e_tot"] = [jnp.exp(t) for t in ltots]
    ltot = jnp.concatenate([jnp.broadcast_to(t, (CHUNK, RWKV_WIDTH)) for t in ltots], axis=0)
    yield

    sl = lambda j: slice(j * PAIR, (j + 1) * PAIR)
    stack = lambda x, dt: [[_stack_pair(x[c * CHUNK:(c + 1) * CHUNK, sl(j)]).astype(dt)
                            for j in range(N_PAIRS)] for c in range(SCAN_SUB)]
    e_neg = jnp.exp(-lc)
    out["b"] = stack(kka * e_neg, bf16)
    yield
    out["k"] = stack(k_dir * e_neg, bf16)
    yield
    e_end = jnp.exp(ltot - lc)
    out["be"] = stack(kka * e_end, bf16)
    yield
    out["ke"] = stack(k_dir * e_end, bf16)
    yield
    out["a"] = stack(-kk * jnp.exp(lc - logw), bf16)
    yield
    out["r"] = stack(r * jnp.exp(lc), f32)
    yield
    out["v"] = stack(v, bf16)


def _scan_units(units, ops, r_s, e_tot, h0, out):
    sr = lax.broadcasted_iota(jnp.int32, (PAIR, PAIR), 0)
    sc = lax.broadcasted_iota(jnp.int32, (PAIR, PAIR), 1)
    same = (sr // CHUNK) == (sc // CHUNK)
    tt, ss = sr % CHUNK, sc % CHUNK
    strict = (same & (ss < tt), same & (ss > tt))
    incl = (same & (ss <= tt), same & (ss >= tt))
    eye = sr == sc
    eye_f = jnp.where(eye, 1.0, 0.0)
    levels = []
    m = 2
    while m < CHUNK:
        levels.append(((sr // (2 * m)) == (sc // (2 * m))) & ((sr // m) != (sc // m)))
        m *= 2
    base = (sr // 2) == (sc // 2)

    nt = ((1,), (1,))
    tn = ((0,), (0,))
    cat0 = lambda u, w: jnp.concatenate([u, w], axis=0)
    cat1 = lambda u, w: jnp.concatenate([u, w], axis=1)
    sl = lambda j: slice(j * PAIR, (j + 1) * PAIR)
    a_s, b_s, k_s, v_s, be_s, ke_s = (ops[n] for n in STASH_NAMES)

    a_ab, a_ak, a_rb, a_rk = [], [], [], []
    for u, (d, _, _) in enumerate(units):
        scores = _dot(cat0(a_s[u], r_s[u].astype(bf16)), cat0(b_s[u], k_s[u]), nt)
        a_ab.append(jnp.where(strict[d], scores[:PAIR, :PAIR], 0.0))
        a_ak.append(jnp.where(strict[d], scores[:PAIR, PAIR:], 0.0).astype(bf16))
        a_rb.append(jnp.where(incl[d], scores[PAIR:, :PAIR], 0.0).astype(bf16))
        a_rk.append(jnp.where(incl[d], scores[PAIR:, PAIR:], 0.0).astype(bf16))
    yield

    t_inv = [eye_f + jnp.where(base, a, 0.0) for a in a_ab]

    akv = [_dot(cat0(x, y), v) for x, y, v in zip(a_ak, a_rk, v_s)]
    av = [x[:PAIR] for x in akv]
    y0b = [x[PAIR:] for x in akv]
    kv = [_dot(k, v, tn) for k, v in zip(ke_s, v_s)]
    yield
    for off in levels:
        tb = [t.astype(bf16) for t in t_inv]
        at = [_dot(jnp.where(off, a, 0.0).astype(bf16), t) for a, t in zip(a_ab, tb)]
        yield
        t_inv = [t + _dot(t16, x.astype(bf16)) for t, t16, x in zip(t_inv, tb, at)]
        yield

    w2 = [_dot(t.astype(bf16), cat1(a, x.astype(bf16))).astype(bf16)
          for t, a, x in zip(t_inv, a_s, av)]
    yield
    qy = [_dot(a, w) for a, w in zip(a_rb, w2)]
    mn = [_dot(b, w, tn) for b, w in zip(be_s, w2)]
    yield
    qh = [cat0(r + x[:, :PAIR], y[:, :PAIR]).astype(bf16) for r, x, y in zip(r_s, qy, mn)]
    yield
    out["y"], out["h"] = {}, dict(h0)
    for pos in range(SCAN_SUB):
        for u, (d, c, j) in enumerate(units):
            if c != (pos if d == 0 else SCAN_SUB - 1 - pos):
                continue
            h = out["h"][d, j]
            qm = _dot(qh[u], h.astype(bf16))
            y_st = qm[:PAIR] + qy[u][:, PAIR:] + y0b[u]
            out["y"][d, c, j] = y_st[:CHUNK] + y_st[CHUNK:]
            decay_col = jnp.sum(jnp.where(eye, e_tot[d][c][:, sl(j)], 0.0), axis=1, keepdims=True)
            out["h"][d, j] = decay_col * h + qm[PAIR:] + mn[u][:, PAIR:] + kv[u]
        yield


def _mixer_kernel(pf_ref, pf_prev_ref, pf_next_ref, pb_ref, pb_prev_ref, pb_next_ref,
                  mu_ref, w0_ref, w2_ref, a0_ref, a2_ref, g2_ref, kk_ref, ka_ref, rk_ref,
                  q_ref, cos_ref, sin_ref, qg_ref, k_ref, v_ref,
                  yf_ref, yb_ref, gate_ref, bonus_ref, att_ref,
                  h_ref, ops_ref, r_ref, e_ref, *, n_blocks):
    s = pl.program_id(1)

    @pl.when(s == 0)
    def _():
        ops_ref[...] = jnp.zeros_like(ops_ref)
        r_ref[...] = jnp.zeros_like(r_ref)
        e_ref[...] = jnp.zeros_like(e_ref)

    @pl.when(s <= 1)
    def _():
        h_ref[...] = jnp.zeros_like(h_ref)

    units = [(d, c, j) for d in range(2) for c in range(SCAN_SUB) for j in range(N_PAIRS)]
    sl = lambda j: slice(j * PAIR, (j + 1) * PAIR)
    rows = lambda c: slice(c * CHUNK, (c + 1) * CHUNK)

    ops = {n: _LazyLoads(lambda u, i=i: ops_ref[(units[u][0], i) + units[u][1:]], len(units))
           for i, n in enumerate(STASH_NAMES)}
    e_tot = [[e_ref[d, c] for c in range(SCAN_SUB)] for d in range(2)]
    h0 = {(d, j): h_ref[d, j] for d in range(2) for j in range(N_PAIRS)}
    scan = {}
    scan_gen = _scan_units(units, ops, _LazyLoads(lambda u: r_ref[units[u]], len(units)),
                           e_tot, h0, scan)

    prm = (mu_ref, w0_ref, w2_ref, a0_ref, a2_ref, g2_ref, kk_ref, ka_ref, rk_ref)
    terms = [{}, {}]
    stages = [scan_gen, _attn_stages(q_ref, cos_ref, sin_ref, qg_ref, k_ref, v_ref, att_ref)]
    for direction, (p_ref, prev_ref, next_ref) in enumerate(
            ((pf_ref, pf_prev_ref, pf_next_ref), (pb_ref, pb_prev_ref, pb_next_ref))):
        bi = jnp.minimum(s, n_blocks - 1) if direction == 0 else jnp.maximum(n_blocks - 1 - s, 0)
        prev_row = jnp.where(bi == 0, 0.0, prev_ref[HALO - 1:HALO, :])
        next_row = jnp.where(bi == n_blocks - 1, 0.0, next_ref[0:1, :])
        stages.append(_chunk_terms(direction, p_ref[...], prev_row, next_row, bi + BLOCK0, prm,
                                   terms[direction]))
    while stages:
        for g in list(stages):
            if next(g, stages) is stages:
                stages.remove(g)

    for (d, j), h in scan["h"].items():
        h_ref[d, j] = h
    for d, c, j in units:
        for i, n in enumerate(STASH_NAMES):
            ops_ref[d, i, c, j] = terms[d][n][c][j]
        r_ref[d, c, j] = terms[d]["r"][c][j]
    for d in range(2):
        for c in range(SCAN_SUB):
            e_ref[d, c] = terms[d]["e_tot"][c]

    @pl.when((s >= 1) & (s <= n_blocks - 1))
    def _():
        gate_ref[...] = terms[0]["gate"]
        bonus_ref[...] = terms[0]["bonus"]

    @pl.when(s >= 2)
    def _():
        for d, c, j in units:
            if d == 0:
                yf_ref[rows(c), sl(j)] = scan["y"][d, c, j]

    @pl.when((s >= 1) & (s <= n_blocks - 1))
    def _():
        for d, c, j in units:
            if d == 1:
                yb_ref[rows(c), sl(j)] = scan["y"][d, c, j]


def _mixers(proj3, n_tok, mu, w0, w2, a0, a2, g2, k_k, k_a, r_k, cos, sin, qg, k4, v4):
    b, lp, _ = proj3.shape
    nb = lp // SCAN_ROWS - BLOCK0
    hb = SCAN_ROWS // HALO
    cb = COL_RWKV // RWKV_IN_PAD
    h_last = (lp // SCAN_ROWS) * hb - 1
    block_of = (lambda s: jnp.minimum(s, nb - 1) + BLOCK0,
                lambda s: jnp.maximum(nb - 1 - s, 0) + BLOCK0)
    main = lambda d: pl.BlockSpec((None, SCAN_ROWS, RWKV_IN_PAD),
                                  lambda bi, s: (bi, block_of[d](s), cb))
    prev = lambda d: pl.BlockSpec((None, HALO, RWKV_IN_PAD),
                                  lambda bi, s: (bi, block_of[d](s) * hb - 1, cb))
    nxt = lambda d: pl.BlockSpec(
        (None, HALO, RWKV_IN_PAD),
        lambda bi, s: (bi, jnp.minimum((block_of[d](s) + 1) * hb, h_last), cb))
    blk = lambda f: pl.BlockSpec((None, SCAN_ROWS, RWKV_WIDTH), lambda bi, s: (bi, f(s), 0))
    y_fwd = blk(lambda s: jnp.maximum(s - 2, 0))
    y_bwd = blk(lambda s: jnp.clip(nb - s - 1, 0, nb - 2))
    prep = blk(lambda s: jnp.clip(s - 1, 0, nb - 2))
    shp = jax.ShapeDtypeStruct((b, n_tok, RWKV_WIDTH), f32)
    params = (mu, w0, w2, a0, a2, g2, k_k, k_a, r_k)
    n_steps = nb + 1
    assert n_steps >= ATTN_KV_HEADS * n_tok // (Q_TILE * Q_SUB)
    attn_in, attn_out = _attn_specs(n_tok, k4.shape[2])
    return pl.pallas_call(
        functools.partial(_mixer_kernel, n_blocks=nb),
        grid=(b, n_steps),
        in_specs=[main(0), prev(0), nxt(0), main(1), prev(1), nxt(1)]
                 + [_resident(a, 2) for a in params] + attn_in,
        out_specs=[y_fwd, y_bwd, prep, prep, attn_out],
        out_shape=[shp, shp, shp, shp, jax.ShapeDtypeStruct((b, n_tok, ATTN_Q_WIDTH), bf16)],
        scratch_shapes=[pltpu.VMEM((2, N_PAIRS, PAIR, PAIR), f32),
                        pltpu.VMEM((2, len(STASH_NAMES), SCAN_SUB, N_PAIRS, PAIR, PAIR), bf16),
                        pltpu.VMEM((2, SCAN_SUB, N_PAIRS, PAIR, PAIR), f32),
                        pltpu.VMEM((2, SCAN_SUB, 1, RWKV_WIDTH), f32)],
        compiler_params=pltpu.CompilerParams(
            dimension_semantics=("parallel", "arbitrary"), vmem_limit_bytes=VMEM_LIMIT),
        name="mixers",
    )(proj3, proj3, proj3, proj3, proj3, proj3, *params, proj3, cos, sin, qg, k4, v4)


def _post_kernel(yf_ref, yb_ref, gate_ref, bonus_ref, att_ref, ga_ref, gb_ref, h_ref,
                 lng_ref, lnb_ref, wa_ref, wb_ref, wo_ref, o_ref):
    gmean = _head_sum_matrix(1.0 / HEAD_DIM)
    y = yf_ref[...] + yb_ref[...]
    mu = _head_sums(y, gmean)
    d = y - mu
    var = _head_sums(d * d, gmean)
    yn = d * lax.rsqrt(var + LNX_EPS) * lng_ref[...] + lnb_ref[...]
    out_a = ((yn + bonus_ref[...]) * gate_ref[...]).astype(bf16)
    ya = jnp.dot(out_a, wa_ref[...], preferred_element_type=f32)
    yb = jnp.dot(att_ref[...], wb_ref[...], preferred_element_type=f32)
    merged = jax.nn.sigmoid(ga_ref[...]) * ya + jax.nn.sigmoid(gb_ref[...]) * yb
    o_ref[...] = h_ref[...] + jnp.dot(merged.astype(bf16), wo_ref[...], preferred_element_type=f32)


def _post(yf, yb, gate, bonus, att, proj3, x, lng, lnb, wa, wb, wo):
    b, n_tok, d = x.shape
    tm = ROW_TILE
    off = FRONT // tm
    row = lambda w: pl.BlockSpec((None, tm, w), lambda bi, i: (bi, i, 0))
    gate_cols = lambda cb: pl.BlockSpec((None, tm, d), lambda bi, i: (bi, i + off, cb))
    return pl.pallas_call(
        _post_kernel,
        grid=(b, n_tok // tm),
        in_specs=[row(RWKV_WIDTH), row(RWKV_WIDTH), row(RWKV_WIDTH), row(RWKV_WIDTH),
                  row(ATTN_Q_WIDTH), gate_cols(COL_GATE_A // d), gate_cols(COL_GATE_B // d), row(d),
                  _resident(lng, 2), _resident(lnb, 2), _resident(wa, 2), _resident(wb, 2),
                  _resident(wo, 2)],
        out_specs=row(d),
        out_shape=jax.ShapeDtypeStruct((b, n_tok, d), f32),
        compiler_params=pltpu.CompilerParams(
            dimension_semantics=("parallel", "parallel"), vmem_limit_bytes=VMEM_LIMIT),
        name="post",
    )(yf, yb, gate, bonus, att, proj3, proj3, x, lng, lnb, wa, wb, wo)


def _ffn_kernel(h_ref, g_ref, w1_ref, w2_ref, gf_ref, o_ref, *, ff_chunk):
    h = h_ref[...]
    ms = jnp.mean(h * h, axis=-1, keepdims=True)
    xn = (h * lax.rsqrt(ms + NORM_EPS) * g_ref[...]).astype(bf16)
    acc = h
    for c in range(0, w1_ref.shape[1], ff_chunk):
        f = jnp.dot(xn, w1_ref[:, c:c + ff_chunk], preferred_element_type=f32)
        f = jnp.square(jnp.maximum(f, 0.0)).astype(bf16)
        acc = acc + jnp.dot(f, w2_ref[c:c + ff_chunk, :], preferred_element_type=f32)
    ms2 = jnp.mean(acc * acc, axis=-1, keepdims=True)
    o_ref[...] = acc * lax.rsqrt(ms2 + NORM_EPS) * gf_ref[...]


def _ffn(h2d, g, w1, w2, gf):
    n, d = h2d.shape
    tm = _pick_tile(n, (1024, ROW_TILE))
    row = pl.BlockSpec((tm, d), lambda i: (i, 0))
    return pl.pallas_call(
        functools.partial(_ffn_kernel, ff_chunk=512),
        grid=(n // tm,),
        in_specs=[row, _resident(g, 1), _resident(w1, 1), _resident(w2, 1), _resident(gf, 1)],
        out_specs=row,
        out_shape=jax.ShapeDtypeStruct((n, d), f32),
        compiler_params=pltpu.CompilerParams(
            dimension_semantics=("parallel",), vmem_limit_bytes=VMEM_LIMIT),
        name="ffn",
    )(h2d, g, w1, w2, gf)


def _rope_tables(n_tok, lk):
    rows = n_tok // GRID_W
    inv_freq = ROPE_THETA ** (-jnp.arange(AXIS_FREQS, dtype=f32) * 2.0 / AXIS_DIM)
    row_ang = jnp.arange(rows, dtype=f32)[:, None] * inv_freq
    col_ang = jnp.arange(GRID_W, dtype=f32)[:, None] * inv_freq
    grid = jnp.stack([jnp.broadcast_to(row_ang[:, None, :], (rows, GRID_W, AXIS_FREQS)),
                      jnp.broadcast_to(col_ang[None, :, :], (rows, GRID_W, AXIS_FREQS))], axis=2)
    grid = grid.reshape(rows * GRID_W, 2, AXIS_FREQS)
    ang = jnp.concatenate([jnp.zeros((lk - n_tok, 2, AXIS_FREQS), f32), grid], axis=0)
    cos, sin = jnp.cos(ang), jnp.sin(ang)
    cos64 = jnp.stack([cos, cos], axis=2).reshape(lk, HEAD_DIM)
    sin64 = jnp.stack([-sin, sin], axis=2).reshape(lk, HEAD_DIM)
    return jnp.tile(cos64, (1, 2)), jnp.tile(sin64, (1, 2))


def kernel(x, meta_tokens, mix_norm_g, w_in, rwkv_shift, decay_w0, decay_w2, icl_a0, icl_a2, gate_w2, k_k, k_a, r_k, lnx_g, lnx_b, q_norm_g, k_norm_g, w_branch_rwkv, w_branch_attn, w_out, ffn_norm_g, w_ff1, w_ff2, final_norm_g):
    b, n_tok, d = x.shape
    assert n_tok % ROW_TILE == 0 and d == COL_GATE_B
    lp = FRONT + n_tok
    lk = lp - KEY_ROW0

    w = w_in[0]
    s0 = RWKV_IN
    s1 = s0 + ATTN_Q_WIDTH
    s2 = s1 + ATTN_KV_WIDTH
    s3 = s2 + ATTN_KV_WIDTH
    s4 = s3 + d
    w_perm = jnp.concatenate(
        [w[:, s3:s4], w[:, s4:], w[:, :s0], jnp.zeros((d, RWKV_IN_PAD - RWKV_IN), w.dtype),
         w[:, s0:s1], w[:, s1:s2], w[:, s2:s3]], axis=1).astype(bf16)
    mu = jnp.pad(rwkv_shift[0], ((0, 0), (0, RWKV_IN_PAD - RWKV_IN)))
    zl = jnp.zeros((DECAY_LORA, RWKV_WIDTH), f32)
    w2cat = jnp.concatenate([jnp.concatenate([decay_w2[0, 0], zl], axis=1),
                             jnp.concatenate([zl, decay_w2[0, 1]], axis=1)], axis=0)
    a2cat = jnp.concatenate([jnp.concatenate([icl_a2[0, 0], zl], axis=1),
                             jnp.concatenate([zl, icl_a2[0, 1]], axis=1)], axis=0)
    w0cat = decay_w0[0].reshape(1, 2 * RWKV_WIDTH)
    a0cat = icl_a0[0].reshape(1, 2 * RWKV_WIDTH)
    g2pad = jnp.pad(gate_w2[0], ((0, GD_PAD - GATE_LORA), (0, 0)))
    row = lambda a: a.reshape(1, -1)

    meta_frame = jnp.concatenate([jnp.zeros((META_ROW0, d), x.dtype), meta_tokens.astype(x.dtype)])
    proj3 = _in_proj(meta_frame, x, row(mix_norm_g[0]), w_perm)

    cos, sin = _rope_tables(n_tok, lk)
    qg = jnp.tile(row(q_norm_g[0]), (1, 2))
    kg = jnp.tile(row(k_norm_g[0]), (1, 2))
    tok0 = FRONT - KEY_ROW0
    k4, v4 = _kv_prep(proj3, cos, sin, kg)
    yf, yb, gate, bonus, att = _mixers(proj3, n_tok, mu, w0cat, w2cat, a0cat, a2cat, g2pad,
                                       row(k_k[0]), row(k_a[0]), row(r_k[0]),
                                       cos[tok0:], sin[tok0:], qg, k4, v4)

    h1 = _post(yf, yb, gate, bonus, att, proj3, x, row(lnx_g[0]), row(lnx_b[0]),
               w_branch_rwkv[0].astype(bf16), w_branch_attn[0].astype(bf16),
               w_out[0].astype(bf16))
    out = _ffn(h1.reshape(b * n_tok, d), row(ffn_norm_g[0]), w_ff1[0].astype(bf16),
               w_ff2[0].astype(bf16), row(final_norm_g))
    return out.reshape(b, n_tok, d)
```

```python
import functools

import jax
import jax.numpy as jnp
from jax import lax
from jax.experimental import pallas as pl
from jax.experimental.pallas import tpu as pltpu

f32 = jnp.float32
bf16 = jnp.bfloat16

N_META = 16
GRID_W = 64
HEAD_DIM = 64
RWKV_HEADS = 8
RWKV_WIDTH = RWKV_HEADS * HEAD_DIM
DECAY_LORA = 64
ICL_LORA = 64
GATE_LORA = 160
LNX_EPS = 64e-5
ATTN_Q_HEADS = 8
ATTN_KV_HEADS = 2
ATTN_GROUP = ATTN_Q_HEADS // ATTN_KV_HEADS
ATTN_Q_WIDTH = ATTN_Q_HEADS * HEAD_DIM
ATTN_KV_WIDTH = ATTN_KV_HEADS * HEAD_DIM
ROPE_THETA = 10000.0
AXIS_DIM = HEAD_DIM // 2
AXIS_FREQS = AXIS_DIM // 2
NORM_EPS = 1e-6

LANES = 128
CHUNK = 64
HALO = 8
PAIR = 2 * HEAD_DIM
N_PAIRS = RWKV_WIDTH // PAIR
ROW_TILE = 512
FRONT = ROW_TILE
META_ROW0 = FRONT - N_META
SCAN_SUB = 2
SCAN_ROWS = SCAN_SUB * CHUNK
BLOCK0 = META_ROW0 // SCAN_ROWS
Q_TILE = 64
Q_SUB = 4
KEY_ROW0 = FRONT - LANES
LOG2E = 1.4426950408889634
MASK_BIAS = -1e30
VMEM_LIMIT = 56 * 1024 * 1024

RWKV_IN = 3 * RWKV_WIDTH + 2 * DECAY_LORA + 2 * ICL_LORA + GATE_LORA
RWKV_IN_PAD = 2048
COL_GATE_A = 0
COL_GATE_B = 1024
COL_RWKV = 2048
COL_Q = 4096
COL_K = 4608
COL_V = 4736
PROJ_W = 4864
OFF_R, OFF_K, OFF_V = 0, RWKV_WIDTH, 2 * RWKV_WIDTH
OFF_WD = 3 * RWKV_WIDTH
OFF_AD = OFF_WD + 2 * DECAY_LORA
OFF_GD = OFF_AD + 2 * ICL_LORA
GD_PAD = RWKV_IN_PAD - OFF_GD


def _pick_tile(n, candidates):
    for c in candidates:
        if n % c == 0:
            return c
    raise ValueError(f"no tile for {n} in {candidates}")


def _split3(x):
    h = x.astype(bf16)
    r = x - h.astype(f32)
    m = r.astype(bf16)
    l = (r - m.astype(f32)).astype(bf16)
    return h, m, l


def _dot(a, b, dims=None):
    if dims is None:
        return jnp.dot(a, b, preferred_element_type=f32)
    return lax.dot_general(a, b, (dims, ((), ())), preferred_element_type=f32)


def _mm(a, b):
    return _dot(a.astype(bf16), b.astype(bf16))


def _mm_exact_lhs(a_bf16, b):
    bh, bm, bl = _split3(b)
    return _dot(a_bf16, bh) + _dot(a_bf16, bm) + _dot(a_bf16, bl)


def _mm_exact_rhs(a, b_bf16):
    ah, am, al = _split3(a)
    return _dot(ah, b_bf16) + _dot(am, b_bf16) + _dot(al, b_bf16)


def _head_sum_matrix(scale):
    r = lax.broadcasted_iota(jnp.int32, (LANES, LANES), 0) // HEAD_DIM
    c = lax.broadcasted_iota(jnp.int32, (LANES, LANES), 1) // HEAD_DIM
    return jnp.where(r == c, scale, 0.0).astype(bf16)


def _head_sums(x, g):
    w = x.shape[-1]
    parts = [_mm_exact_rhs(x[:, i:i + LANES], g) for i in range(0, w, LANES)]
    return parts[0] if len(parts) == 1 else jnp.concatenate(parts, axis=-1)


def _resident(a, n_grid):
    zeros = (0,) * a.ndim
    return pl.BlockSpec(a.shape, lambda *_: zeros, pipeline_mode=pl.Buffered(1))


def _in_proj_kernel(mf_ref, x_ref, g_ref, w_ref, o_ref):
    first = pl.program_id(1) == 0
    half = ROW_TILE // 2
    us = []
    for r in (slice(0, half), slice(half, ROW_TILE)):
        x = jnp.where(first, mf_ref[r, :], x_ref[r, :])
        ms = jnp.mean(x * x, axis=-1, keepdims=True)
        us.append((x * lax.rsqrt(ms + NORM_EPS) * g_ref[...]).astype(bf16))
    for r, u in zip((slice(0, half), slice(half, ROW_TILE)), us):
        o_ref[r, :] = jnp.dot(u, w_ref[...], preferred_element_type=f32)


def _in_proj(meta_frame, x, g, w):
    b, n_tok, d = x.shape
    nx = n_tok // ROW_TILE
    return pl.pallas_call(
        _in_proj_kernel,
        grid=(b, nx + 1),
        in_specs=[_resident(meta_frame, 2),
                  pl.BlockSpec((None, ROW_TILE, d), lambda bi, i: (bi, jnp.maximum(i - 1, 0), 0)),
                  _resident(g, 2), _resident(w, 2)],
        out_specs=pl.BlockSpec((None, ROW_TILE, PROJ_W), lambda bi, i: (bi, i, 0)),
        out_shape=jax.ShapeDtypeStruct((b, FRONT + n_tok, PROJ_W), f32),
        compiler_params=pltpu.CompilerParams(
            dimension_semantics=("parallel", "arbitrary"), vmem_limit_bytes=VMEM_LIMIT),
        name="in_proj",
    )(meta_frame, x, g, w)


def _norm_rope(x, gain, cos, sin_signed, gsum, scale):
    ms = _mm_exact_rhs(x * x, gsum)
    xn = x * lax.rsqrt(ms + NORM_EPS) * gain
    lane = lax.broadcasted_iota(jnp.int32, x.shape, 1)
    first_half = (lane % AXIS_DIM) < AXIS_FREQS
    partner = jnp.where(first_half,
                        pltpu.roll(xn, LANES - AXIS_FREQS, axis=1),
                        pltpu.roll(xn, AXIS_FREQS, axis=1))
    out = xn * cos + partner * sin_signed
    return out * scale if scale != 1.0 else out


def _kv_kernel(k_ref, v_ref, cos_ref, sin_ref, kg_ref, ko_ref, vo_ref):
    gsum = _head_sum_matrix(1.0 / HEAD_DIM)
    kn = _norm_rope(k_ref[...], kg_ref[...], cos_ref[...], sin_ref[...], gsum, 1.0)
    vf = v_ref[...]
    t = kn.shape[0]
    lane = lax.broadcasted_iota(jnp.int32, kn.shape, 1)
    key_row = lax.broadcasted_iota(jnp.int32, kn.shape, 0) + pl.program_id(1) * t
    extra = lane == HEAD_DIM
    k_aug = jnp.where(extra & (key_row < META_ROW0 - KEY_ROW0), MASK_BIAS, 0.0)
    v_aug = jnp.where(extra, 1.0, 0.0)
    for h in range(ATTN_KV_HEADS):
        kh = kn if h == 0 else pltpu.roll(kn, HEAD_DIM, axis=1)
        vh = vf if h == 0 else pltpu.roll(vf, HEAD_DIM, axis=1)
        ko_ref[h] = jnp.where(lane < HEAD_DIM, kh, k_aug).astype(bf16)
        vo_ref[h] = jnp.where(lane < HEAD_DIM, vh, v_aug).astype(bf16)


def _kv_prep(proj3, cos, sin, kg):
    b, lp, _ = proj3.shape
    lk = lp - KEY_ROW0
    t = _pick_tile(lk, (KEY_ROW0, LANES))
    off = KEY_ROW0 // t
    row = lambda w, cb: pl.BlockSpec((None, t, w), lambda bi, ti: (bi, ti + off, cb))
    tab = pl.BlockSpec((t, LANES), lambda bi, ti: (ti, 0))
    vec = pl.BlockSpec((1, LANES), lambda bi, ti: (0, 0))
    kv_out = pl.BlockSpec((None, ATTN_KV_HEADS, t, LANES), lambda bi, ti: (bi, 0, ti, 0))
    kv_shape = jax.ShapeDtypeStruct((b, ATTN_KV_HEADS, lk, LANES), bf16)
    return pl.pallas_call(
        _kv_kernel,
        grid=(b, lk // t),
        in_specs=[row(ATTN_KV_WIDTH, COL_K // ATTN_KV_WIDTH),
                  row(ATTN_KV_WIDTH, COL_V // ATTN_KV_WIDTH),
                  tab, tab, vec],
        out_specs=[kv_out, kv_out],
        out_shape=[kv_shape, kv_shape],
        compiler_params=pltpu.CompilerParams(dimension_semantics=("parallel", "parallel")),
        name="kv_prep",
    )(proj3, proj3, cos, sin, kg)


def _attn_stages(q_ref, cos_ref, sin_ref, qg_ref, k_ref, v_ref, o_ref):
    tq = Q_TILE
    lane = lax.broadcasted_iota(jnp.int32, (tq, LANES), 1)
    one_hot = jnp.where(lane == HEAD_DIM, 1.0, 0.0)
    n_sub = q_ref.shape[0] // tq
    gsum = _head_sum_matrix(1.0 / HEAD_DIM)
    qn = [_norm_rope(q_ref[:, i:i + LANES], qg_ref[...], cos_ref[...], sin_ref[...], gsum,
                     HEAD_DIM ** -0.5 * LOG2E) for i in range(0, ATTN_GROUP * HEAD_DIM, LANES)]
    yield

    def scores(i):
        rows = []
        for g in range(ATTN_GROUP):
            blk = qn[g // 2][i * tq:(i + 1) * tq]
            if g % 2:
                blk = pltpu.roll(blk, HEAD_DIM, axis=1)
            rows.append(jnp.where(lane < HEAD_DIM, blk, one_hot))
        q = jnp.concatenate(rows, axis=0).astype(bf16)
        return lax.dot_general(q, k_ref[...], (((1,), (1,)), ((), ())), preferred_element_type=f32)

    def probs(s):
        return jnp.exp2(s - jnp.max(s, axis=-1, keepdims=True)).astype(bf16)

    def finish(i, p):
        ov = jnp.dot(p, v_ref[...], preferred_element_type=f32)
        on = ov / ov[:, HEAD_DIM:HEAD_DIM + 1]
        for g in range(0, ATTN_GROUP, 2):
            even = on[g * tq:(g + 1) * tq]
            odd = pltpu.roll(on[(g + 1) * tq:(g + 2) * tq], HEAD_DIM, axis=1)
            o_ref[i * tq:(i + 1) * tq, (g // 2) * LANES:(g // 2 + 1) * LANES] = jnp.where(
                lane < HEAD_DIM, even, odd).astype(o_ref.dtype)

    s, p = {}, {}
    for t in range(n_sub + 2):
        if t < n_sub:
            s[t] = scores(t)
        if 0 <= t - 1 < n_sub:
            p[t - 1] = probs(s.pop(t - 1))
        if 0 <= t - 2 < n_sub:
            finish(t - 2, p.pop(t - 2))
        yield


def _attn_specs(n_tok, lk):
    gw = ATTN_GROUP * HEAD_DIM
    tq = Q_TILE * Q_SUB
    nq = n_tok // tq
    q_row0, q_col0 = FRONT // tq, COL_Q // gw
    tile = lambda s: jnp.minimum(s, ATTN_KV_HEADS * nq - 1)
    head = lambda s: tile(s) // nq
    qi = lambda s: tile(s) % nq
    tab = pl.BlockSpec((tq, LANES), lambda bi, s: (qi(s), 0))
    kv = pl.BlockSpec((None, None, lk, LANES), lambda bi, s: (bi, head(s), 0, 0))
    ins = [pl.BlockSpec((None, tq, gw), lambda bi, s: (bi, qi(s) + q_row0, head(s) + q_col0)),
           tab, tab, pl.BlockSpec((1, LANES), lambda bi, s: (0, 0)), kv, kv]
    out = pl.BlockSpec((None, tq, gw), lambda bi, s: (bi, qi(s), head(s)))
    return ins, out


def _shifted_rows(p, prev_row, next_row):
    rows = lax.broadcasted_iota(jnp.int32, p.shape, 0)
    n = p.shape[0]
    prev = jnp.where(rows == 0, prev_row, pltpu.roll(p, 1, axis=0))
    nxt = jnp.where(rows == n - 1, next_row, pltpu.roll(p, n - 1, axis=0))
    return prev, nxt


def _stack_pair(x):
    lane = lax.broadcasted_iota(jnp.int32, x.shape, 1)
    first = lane < HEAD_DIM
    return jnp.concatenate([jnp.where(first, x, 0.0), jnp.where(first, 0.0, x)], axis=0)


STASH_NAMES = ("a", "b", "k", "v", "be", "ke")


class _LazyLoads:
    def __init__(self, load, n):
        self._load, self._n = load, n

    def __len__(self):
        return self._n

    def __getitem__(self, u):
        return self._load(u)

    def __iter__(self):
        return (self._load(u) for u in range(self._n))


def _chunk_terms(direction, p, prev_row, next_row, chunk_idx, prm, out):
    (mu_ref, w0_ref, w2_ref, a0_ref, a2_ref, g2_ref, kk_ref, ka_ref, rk_ref) = prm
    mu = mu_ref[...]
    prev, nxt = _shifted_rows(p, prev_row, next_row)
    z = p + mu[0:1] * (prev - p) + mu[1:2] * (nxt - p)

    rows = lax.broadcasted_iota(jnp.int32, (SCAN_ROWS, 1), 0) + chunk_idx * SCAN_ROWS
    valid = rows >= META_ROW0
    z = jnp.where(valid, z, 0.0)
    yield

    r = z[:, OFF_R:OFF_R + RWKV_WIDTH]
    k = z[:, OFF_K:OFF_K + RWKV_WIDTH]
    v = z[:, OFF_V:OFF_V + RWKV_WIDTH]
    lo, hi = direction * RWKV_WIDTH, (direction + 1) * RWKV_WIDTH

    x_w = w0_ref[...] + _mm(jnp.tanh(z[:, OFF_WD:OFF_AD]), w2_ref[...])
    logw = -jnp.exp(f32(-0.5)) * jax.nn.sigmoid(x_w[:, lo:hi])
    logw = jnp.where(valid, logw, 0.0)
    a_all = jax.nn.sigmoid(a0_ref[...] + _mm(z[:, OFF_AD:OFF_GD], a2_ref[...]))
    a_dir = a_all[:, lo:hi]
    yield

    gsum = _head_sum_matrix(1.0)
    kk = k * kk_ref[...]
    kk = kk / jnp.maximum(jnp.sqrt(_head_sums(kk * kk, gsum)), 1e-12)
    k_a = ka_ref[...]
    k_dir = k * (1.0 + (a_dir - 1.0) * k_a)
    kka = kk * a_dir
    yield

    if direction == 0:
        out["gate"] = _mm(jax.nn.sigmoid(z[:, OFF_GD:RWKV_IN_PAD]), g2_ref[...])
        a_mean = 0.5 * (a_all[:, :RWKV_WIDTH] + a_all[:, RWKV_WIDTH:])
        k_mean = k * (1.0 + (a_mean - 1.0) * k_a)
        out["bonus"] = _head_sums(r * k_mean * rk_ref[...], gsum) * v
        yield

    tr = lax.broadcasted_iota(jnp.int32, (SCAN_ROWS, SCAN_ROWS), 0)
    tc = lax.broadcasted_iota(jnp.int32, (SCAN_ROWS, SCAN_ROWS), 1)
    tri = ((tr // CHUNK) == (tc // CHUNK)) & ((tc <= tr) if direction == 0 else (tc >= tr))
    lc = _mm_exact_lhs(jnp.where(tri, 1.0, 0.0).astype(bf16), logw)
    last = CHUNK - 1 if direction == 0 else 0
    ltots = [lc[c * CHUNK + last:c * CHUNK + last + 1, :] for c in range(SCAN_SUB)]
    out["e_tot"] = [jnp.exp(t) for t in ltots]
    ltot = jnp.concatenate([jnp.broadcast_to(t, (CHUNK, RWKV_WIDTH)) for t in ltots], axis=0)
    yield

    sl = lambda j: slice(j * PAIR, (j + 1) * PAIR)
    stack = lambda x, dt: [[_stack_pair(x[c * CHUNK:(c + 1) * CHUNK, sl(j)]).astype(dt)
                            for j in range(N_PAIRS)] for c in range(SCAN_SUB)]
    e_neg = jnp.exp(-lc)
    out["b"] = stack(kka * e_neg, bf16)
    yield
    out["k"] = stack(k_dir * e_neg, bf16)
    yield
    e_end = jnp.exp(ltot - lc)
    out["be"] = stack(kka * e_end, bf16)
    yield
    out["ke"] = stack(k_dir * e_end, bf16)
    yield
    out["a"] = stack(-kk * jnp.exp(lc - logw), bf16)
    yield
    out["r"] = stack(r * jnp.exp(lc), f32)
    yield
    out["v"] = stack(v, bf16)


def _scan_units(units, ops, r_s, e_tot, h0, out):
    sr = lax.broadcasted_iota(jnp.int32, (PAIR, PAIR), 0)
    sc = lax.broadcasted_iota(jnp.int32, (PAIR, PAIR), 1)
    same = (sr // CHUNK) == (sc // CHUNK)
    tt, ss = sr % CHUNK, sc % CHUNK
    strict = (same & (ss < tt), same & (ss > tt))
    incl = (same & (ss <= tt), same & (ss >= tt))
    eye = sr == sc
    eye_f = jnp.where(eye, 1.0, 0.0)
    levels = []
    m = 2
    while m < CHUNK:
        levels.append(((sr // (2 * m)) == (sc // (2 * m))) & ((sr // m) != (sc // m)))
        m *= 2
    base = (sr // 2) == (sc // 2)

    nt = ((1,), (1,))
    tn = ((0,), (0,))
    cat0 = lambda u, w: jnp.concatenate([u, w], axis=0)
    cat1 = lambda u, w: jnp.concatenate([u, w], axis=1)
    sl = lambda j: slice(j * PAIR, (j + 1) * PAIR)
    a_s, b_s, k_s, v_s, be_s, ke_s = (ops[n] for n in STASH_NAMES)

    a_ab, a_ak, a_rb, a_rk = [], [], [], []
    for u, (d, _, _) in enumerate(units):
        scores = _dot(cat0(a_s[u], r_s[u].astype(bf16)), cat0(b_s[u], k_s[u]), nt)
        a_ab.append(jnp.where(strict[d], scores[:PAIR, :PAIR], 0.0))
        a_ak.append(jnp.where(strict[d], scores[:PAIR, PAIR:], 0.0).astype(bf16))
        a_rb.append(jnp.where(incl[d], scores[PAIR:, :PAIR], 0.0).astype(bf16))
        a_rk.append(jnp.where(incl[d], scores[PAIR:, PAIR:], 0.0).astype(bf16))
    yield

    t_inv = [eye_f + jnp.where(base, a, 0.0) for a in a_ab]

    akv = [_dot(cat0(x, y), v) for x, y, v in zip(a_ak, a_rk, v_s)]
    av = [x[:PAIR] for x in akv]
    y0b = [x[PAIR:] for x in akv]
    kv = [_dot(k, v, tn) for k, v in zip(ke_s, v_s)]
    yield
    for off in levels:
        tb = [t.astype(bf16) for t in t_inv]
        at = [_dot(jnp.where(off, a, 0.0).astype(bf16), t) for a, t in zip(a_ab, tb)]
        yield
        t_inv = [t + _dot(t16, x.astype(bf16)) for t, t16, x in zip(t_inv, tb, at)]
        yield

    w2 = [_dot(t.astype(bf16), cat1(a, x.astype(bf16))).astype(bf16)
          for t, a, x in zip(t_inv, a_s, av)]
    yield
    qy = [_dot(a, w) for a, w in zip(a_rb, w2)]
    mn = [_dot(b, w, tn) for b, w in zip(be_s, w2)]
    yield
    qh = [cat0(r + x[:, :PAIR], y[:, :PAIR]).astype(bf16) for r, x, y in zip(r_s, qy, mn)]
    yield
    out["y"], out["h"] = {}, dict(h0)
    for pos in range(SCAN_SUB):
        for u, (d, c, j) in enumerate(units):
            if c != (pos if d == 0 else SCAN_SUB - 1 - pos):
                continue
            h = out["h"][d, j]
            qm = _dot(qh[u], h.astype(bf16))
            y_st = qm[:PAIR] + qy[u][:, PAIR:] + y0b[u]
            out["y"][d, c, j] = y_st[:CHUNK] + y_st[CHUNK:]
            decay_col = jnp.sum(jnp.where(eye, e_tot[d][c][:, sl(j)], 0.0), axis=1, keepdims=True)
            out["h"][d, j] = decay_col * h + qm[PAIR:] + mn[u][:, PAIR:] + kv[u]
        yield


def _mixer_kernel(pf_ref, pf_prev_ref, pf_next_ref, pb_ref, pb_prev_ref, pb_next_ref,
                  mu_ref, w0_ref, w2_ref, a0_ref, a2_ref, g2_ref, kk_ref, ka_ref, rk_ref,
                  q_ref, cos_ref, sin_ref, qg_ref, k_ref, v_ref,
                  yf_ref, yb_ref, gate_ref, bonus_ref, att_ref,
                  h_ref, ops_ref, r_ref, e_ref, *, n_blocks):
    s = pl.program_id(1)

    @pl.when(s == 0)
    def _():
        ops_ref[...] = jnp.zeros_like(ops_ref)
        r_ref[...] = jnp.zeros_like(r_ref)
        e_ref[...] = jnp.zeros_like(e_ref)

    @pl.when(s <= 1)
    def _():
        h_ref[...] = jnp.zeros_like(h_ref)

    units = [(d, c, j) for d in range(2) for c in range(SCAN_SUB) for j in range(N_PAIRS)]
    sl = lambda j: slice(j * PAIR, (j + 1) * PAIR)
    rows = lambda c: slice(c * CHUNK, (c + 1) * CHUNK)

    ops = {n: _LazyLoads(lambda u, i=i: ops_ref[(units[u][0], i) + units[u][1:]], len(units))
           for i, n in enumerate(STASH_NAMES)}
    e_tot = [[e_ref[d, c] for c in range(SCAN_SUB)] for d in range(2)]
    h0 = {(d, j): h_ref[d, j] for d in range(2) for j in range(N_PAIRS)}
    scan = {}
    scan_gen = _scan_units(units, ops, _LazyLoads(lambda u: r_ref[units[u]], len(units)),
                           e_tot, h0, scan)

    prm = (mu_ref, w0_ref, w2_ref, a0_ref, a2_ref, g2_ref, kk_ref, ka_ref, rk_ref)
    terms = [{}, {}]
    stages = [scan_gen, _attn_stages(q_ref, cos_ref, sin_ref, qg_ref, k_ref, v_ref, att_ref)]
    for direction, (p_ref, prev_ref, next_ref) in enumerate(
            ((pf_ref, pf_prev_ref, pf_next_ref), (pb_ref, pb_prev_ref, pb_next_ref))):
        bi = jnp.minimum(s, n_blocks - 1) if direction == 0 else jnp.maximum(n_blocks - 1 - s, 0)
        prev_row = jnp.where(bi == 0, 0.0, prev_ref[HALO - 1:HALO, :])
        next_row = jnp.where(bi == n_blocks - 1, 0.0, next_ref[0:1, :])
        stages.append(_chunk_terms(direction, p_ref[...], prev_row, next_row, bi + BLOCK0, prm,
                                   terms[direction]))
    while stages:
        for g in list(stages):
            if next(g, stages) is stages:
                stages.remove(g)

    for (d, j), h in scan["h"].items():
        h_ref[d, j] = h
    for d, c, j in units:
        for i, n in enumerate(STASH_NAMES):
            ops_ref[d, i, c, j] = terms[d][n][c][j]
        r_ref[d, c, j] = terms[d]["r"][c][j]
    for d in range(2):
        for c in range(SCAN_SUB):
            e_ref[d, c] = terms[d]["e_tot"][c]

    @pl.when((s >= 1) & (s <= n_blocks - 1))
    def _():
        gate_ref[...] = terms[0]["gate"].astype(gate_ref.dtype)
        bonus_ref[...] = terms[0]["bonus"].astype(bonus_ref.dtype)

    @pl.when(s >= 2)
    def _():
        for d, c, j in units:
            if d == 0:
                yf_ref[rows(c), sl(j)] = scan["y"][d, c, j].astype(yf_ref.dtype)

    @pl.when((s >= 1) & (s <= n_blocks - 1))
    def _():
        for d, c, j in units:
            if d == 1:
                yb_ref[rows(c), sl(j)] = scan["y"][d, c, j].astype(yb_ref.dtype)


def _mixers(proj3, n_tok, mu, w0, w2, a0, a2, g2, k_k, k_a, r_k, cos, sin, qg, k4, v4):
    b, lp, _ = proj3.shape
    nb = lp // SCAN_ROWS - BLOCK0
    hb = SCAN_ROWS // HALO
    cb = COL_RWKV // RWKV_IN_PAD
    h_last = (lp // SCAN_ROWS) * hb - 1
    block_of = (lambda s: jnp.minimum(s, nb - 1) + BLOCK0,
                lambda s: jnp.maximum(nb - 1 - s, 0) + BLOCK0)
    main = lambda d: pl.BlockSpec((None, SCAN_ROWS, RWKV_IN_PAD),
                                  lambda bi, s: (bi, block_of[d](s), cb))
    prev = lambda d: pl.BlockSpec((None, HALO, RWKV_IN_PAD),
                                  lambda bi, s: (bi, block_of[d](s) * hb - 1, cb))
    nxt = lambda d: pl.BlockSpec(
        (None, HALO, RWKV_IN_PAD),
        lambda bi, s: (bi, jnp.minimum((block_of[d](s) + 1) * hb, h_last), cb))
    blk = lambda f: pl.BlockSpec((None, SCAN_ROWS, RWKV_WIDTH), lambda bi, s: (bi, f(s), 0))
    y_fwd = blk(lambda s: jnp.maximum(s - 2, 0))
    y_bwd = blk(lambda s: jnp.clip(nb - s - 1, 0, nb - 2))
    prep = blk(lambda s: jnp.clip(s - 1, 0, nb - 2))
    shp = jax.ShapeDtypeStruct((b, n_tok, RWKV_WIDTH), bf16)
    params = (mu, w0, w2, a0, a2, g2, k_k, k_a, r_k)
    n_steps = nb + 1
    assert n_steps >= ATTN_KV_HEADS * n_tok // (Q_TILE * Q_SUB)
    attn_in, attn_out = _attn_specs(n_tok, k4.shape[2])
    return pl.pallas_call(
        functools.partial(_mixer_kernel, n_blocks=nb),
        grid=(b, n_steps),
        in_specs=[main(0), prev(0), nxt(0), main(1), prev(1), nxt(1)]
                 + [_resident(a, 2) for a in params] + attn_in,
        out_specs=[y_fwd, y_bwd, prep, prep, attn_out],
        out_shape=[shp, shp, shp, shp, jax.ShapeDtypeStruct((b, n_tok, ATTN_Q_WIDTH), bf16)],
        scratch_shapes=[pltpu.VMEM((2, N_PAIRS, PAIR, PAIR), f32),
                        pltpu.VMEM((2, len(STASH_NAMES), SCAN_SUB, N_PAIRS, PAIR, PAIR), bf16),
                        pltpu.VMEM((2, SCAN_SUB, N_PAIRS, PAIR, PAIR), f32),
                        pltpu.VMEM((2, SCAN_SUB, 1, RWKV_WIDTH), f32)],
        compiler_params=pltpu.CompilerParams(
            dimension_semantics=("parallel", "arbitrary"), vmem_limit_bytes=VMEM_LIMIT),
        name="mixers",
    )(proj3, proj3, proj3, proj3, proj3, proj3, *params, proj3, cos, sin, qg, k4, v4)


def _post_kernel(yf_ref, yb_ref, gate_ref, bonus_ref, att_ref, ga_ref, gb_ref, h_ref,
                 lng_ref, lnb_ref, wa_ref, wb_ref, wo_ref, o_ref):
    gmean = _head_sum_matrix(1.0 / HEAD_DIM)
    y = yf_ref[...].astype(f32) + yb_ref[...].astype(f32)
    mu = _head_sums(y, gmean)
    d = y - mu
    var = _head_sums(d * d, gmean)
    yn = d * lax.rsqrt(var + LNX_EPS) * lng_ref[...] + lnb_ref[...]
    out_a = ((yn + bonus_ref[...].astype(f32)) * gate_ref[...].astype(f32)).astype(bf16)
    ya = jnp.dot(out_a, wa_ref[...], preferred_element_type=f32)
    yb = jnp.dot(att_ref[...], wb_ref[...], preferred_element_type=f32)
    merged = jax.nn.sigmoid(ga_ref[...]) * ya + jax.nn.sigmoid(gb_ref[...]) * yb
    o_ref[...] = h_ref[...] + jnp.dot(merged.astype(bf16), wo_ref[...], preferred_element_type=f32)


def _post(yf, yb, gate, bonus, att, proj3, x, lng, lnb, wa, wb, wo):
    b, n_tok, d = x.shape
    tm = ROW_TILE
    off = FRONT // tm
    row = lambda w: pl.BlockSpec((None, tm, w), lambda bi, i: (bi, i, 0))
    gate_cols = lambda cb: pl.BlockSpec((None, tm, d), lambda bi, i: (bi, i + off, cb))
    return pl.pallas_call(
        _post_kernel,
        grid=(b, n_tok // tm),
        in_specs=[row(RWKV_WIDTH), row(RWKV_WIDTH), row(RWKV_WIDTH), row(RWKV_WIDTH),
                  row(ATTN_Q_WIDTH), gate_cols(COL_GATE_A // d), gate_cols(COL_GATE_B // d), row(d),
                  _resident(lng, 2), _resident(lnb, 2), _resident(wa, 2), _resident(wb, 2),
                  _resident(wo, 2)],
        out_specs=row(d),
        out_shape=jax.ShapeDtypeStruct((b, n_tok, d), f32),
        compiler_params=pltpu.CompilerParams(
            dimension_semantics=("parallel", "parallel"), vmem_limit_bytes=VMEM_LIMIT),
        name="post",
    )(yf, yb, gate, bonus, att, proj3, proj3, x, lng, lnb, wa, wb, wo)


def _ffn_kernel(h_ref, g_ref, w1_ref, w2_ref, gf_ref, o_ref, *, ff_chunk):
    h = h_ref[...]
    ms = jnp.mean(h * h, axis=-1, keepdims=True)
    xn = (h * lax.rsqrt(ms + NORM_EPS) * g_ref[...]).astype(bf16)
    acc = h
    for c in range(0, w1_ref.shape[1], ff_chunk):
        f = jnp.dot(xn, w1_ref[:, c:c + ff_chunk], preferred_element_type=f32)
        f = jnp.square(jnp.maximum(f, 0.0)).astype(bf16)
        acc = acc + jnp.dot(f, w2_ref[c:c + ff_chunk, :], preferred_element_type=f32)
    ms2 = jnp.mean(acc * acc, axis=-1, keepdims=True)
    o_ref[...] = acc * lax.rsqrt(ms2 + NORM_EPS) * gf_ref[...]


def _ffn(h2d, g, w1, w2, gf):
    n, d = h2d.shape
    tm = _pick_tile(n, (1024, ROW_TILE))
    row = pl.BlockSpec((tm, d), lambda i: (i, 0))
    return pl.pallas_call(
        functools.partial(_ffn_kernel, ff_chunk=512),
        grid=(n // tm,),
        in_specs=[row, _resident(g, 1), _resident(w1, 1), _resident(w2, 1), _resident(gf, 1)],
        out_specs=row,
        out_shape=jax.ShapeDtypeStruct((n, d), f32),
        compiler_params=pltpu.CompilerParams(
            dimension_semantics=("parallel",), vmem_limit_bytes=VMEM_LIMIT),
        name="ffn",
    )(h2d, g, w1, w2, gf)


def _rope_tables(n_tok, lk):
    rows = n_tok // GRID_W
    inv_freq = ROPE_THETA ** (-jnp.arange(AXIS_FREQS, dtype=f32) * 2.0 / AXIS_DIM)
    row_ang = jnp.arange(rows, dtype=f32)[:, None] * inv_freq
    col_ang = jnp.arange(GRID_W, dtype=f32)[:, None] * inv_freq
    grid = jnp.stack([jnp.broadcast_to(row_ang[:, None, :], (rows, GRID_W, AXIS_FREQS)),
                      jnp.broadcast_to(col_ang[None, :, :], (rows, GRID_W, AXIS_FREQS))], axis=2)
    grid = grid.reshape(rows * GRID_W, 2, AXIS_FREQS)
    ang = jnp.concatenate([jnp.zeros((lk - n_tok, 2, AXIS_FREQS), f32), grid], axis=0)
    cos, sin = jnp.cos(ang), jnp.sin(ang)
    cos64 = jnp.stack([cos, cos], axis=2).reshape(lk, HEAD_DIM)
    sin64 = jnp.stack([-sin, sin], axis=2).reshape(lk, HEAD_DIM)
    return jnp.tile(cos64, (1, 2)), jnp.tile(sin64, (1, 2))


def kernel(x, meta_tokens, mix_norm_g, w_in, rwkv_shift, decay_w0, decay_w2, icl_a0, icl_a2, gate_w2, k_k, k_a, r_k, lnx_g, lnx_b, q_norm_g, k_norm_g, w_branch_rwkv, w_branch_attn, w_out, ffn_norm_g, w_ff1, w_ff2, final_norm_g):
    b, n_tok, d = x.shape
    assert n_tok % ROW_TILE == 0 and d == COL_GATE_B
    lp = FRONT + n_tok
    lk = lp - KEY_ROW0

    w = w_in[0]
    s0 = RWKV_IN
    s1 = s0 + ATTN_Q_WIDTH
    s2 = s1 + ATTN_KV_WIDTH
    s3 = s2 + ATTN_KV_WIDTH
    s4 = s3 + d
    w_perm = jnp.concatenate(
        [w[:, s3:s4], w[:, s4:], w[:, :s0], jnp.zeros((d, RWKV_IN_PAD - RWKV_IN), w.dtype),
         w[:, s0:s1], w[:, s1:s2], w[:, s2:s3]], axis=1).astype(bf16)
    mu = jnp.pad(rwkv_shift[0], ((0, 0), (0, RWKV_IN_PAD - RWKV_IN)))
    zl = jnp.zeros((DECAY_LORA, RWKV_WIDTH), f32)
    w2cat = jnp.concatenate([jnp.concatenate([decay_w2[0, 0], zl], axis=1),
                             jnp.concatenate([zl, decay_w2[0, 1]], axis=1)], axis=0)
    a2cat = jnp.concatenate([jnp.concatenate([icl_a2[0, 0], zl], axis=1),
                             jnp.concatenate([zl, icl_a2[0, 1]], axis=1)], axis=0)
    w0cat = decay_w0[0].reshape(1, 2 * RWKV_WIDTH)
    a0cat = icl_a0[0].reshape(1, 2 * RWKV_WIDTH)
    g2pad = jnp.pad(gate_w2[0], ((0, GD_PAD - GATE_LORA), (0, 0)))
    row = lambda a: a.reshape(1, -1)

    meta_frame = jnp.concatenate([jnp.zeros((META_ROW0, d), x.dtype), meta_tokens.astype(x.dtype)])
    proj3 = _in_proj(meta_frame, x, row(mix_norm_g[0]), w_perm)

    cos, sin = _rope_tables(n_tok, lk)
    qg = jnp.tile(row(q_norm_g[0]), (1, 2))
    kg = jnp.tile(row(k_norm_g[0]), (1, 2))
    tok0 = FRONT - KEY_ROW0
    k4, v4 = _kv_prep(proj3, cos, sin, kg)
    yf, yb, gate, bonus, att = _mixers(proj3, n_tok, mu, w0cat, w2cat, a0cat, a2cat, g2pad,
                                       row(k_k[0]), row(k_a[0]), row(r_k[0]),
                                       cos[tok0:], sin[tok0:], qg, k4, v4)

    h1 = _post(yf, yb, gate, bonus, att, proj3, x, row(lnx_g[0]), row(lnx_b[0]),
               w_branch_rwkv[0].astype(bf16), w_branch_attn[0].astype(bf16),
               w_out[0].astype(bf16))
    out = _ffn(h1.reshape(b * n_tok, d), row(ffn_norm_g[0]), w_ff1[0].astype(bf16),
               w_ff2[0].astype(bf16), row(final_norm_g))
    return out.reshape(b, n_tok, d)
```

```python
import functools

import jax
import jax.numpy as jnp
from jax import lax
from jax.experimental import pallas as pl
from jax.experimental.pallas import tpu as pltpu

f32 = jnp.float32
bf16 = jnp.bfloat16

N_META = 16
GRID_W = 64
HEAD_DIM = 64
RWKV_HEADS = 8
RWKV_WIDTH = RWKV_HEADS * HEAD_DIM
DECAY_LORA = 64
ICL_LORA = 64
GATE_LORA = 160
LNX_EPS = 64e-5
ATTN_Q_HEADS = 8
ATTN_KV_HEADS = 2
ATTN_GROUP = ATTN_Q_HEADS // ATTN_KV_HEADS
ATTN_Q_WIDTH = ATTN_Q_HEADS * HEAD_DIM
ATTN_KV_WIDTH = ATTN_KV_HEADS * HEAD_DIM
ROPE_THETA = 10000.0
AXIS_DIM = HEAD_DIM // 2
AXIS_FREQS = AXIS_DIM // 2
NORM_EPS = 1e-6

LANES = 128
CHUNK = 64
HALO = 8
PAIR = 2 * HEAD_DIM
N_PAIRS = RWKV_WIDTH // PAIR
ROW_TILE = 512
FRONT = ROW_TILE
META_ROW0 = FRONT - N_META
SCAN_SUB = 2
SCAN_ROWS = SCAN_SUB * CHUNK
BLOCK0 = META_ROW0 // SCAN_ROWS
Q_TILE = 64
Q_SUB = 4
KEY_ROW0 = FRONT - LANES
LOG2E = 1.4426950408889634
MASK_BIAS = -1e30
VMEM_LIMIT = 56 * 1024 * 1024

RWKV_IN = 3 * RWKV_WIDTH + 2 * DECAY_LORA + 2 * ICL_LORA + GATE_LORA
RWKV_IN_PAD = 2048
COL_GATE_A = 0
COL_GATE_B = 1024
COL_RWKV = 2048
COL_Q = 4096
COL_K = 4608
COL_V = 4736
PROJ_W = 4864
OFF_R, OFF_K, OFF_V = 0, RWKV_WIDTH, 2 * RWKV_WIDTH
OFF_WD = 3 * RWKV_WIDTH
OFF_AD = OFF_WD + 2 * DECAY_LORA
OFF_GD = OFF_AD + 2 * ICL_LORA
GD_PAD = RWKV_IN_PAD - OFF_GD


def _pick_tile(n, candidates):
    for c in candidates:
        if n % c == 0:
            return c
    raise ValueError(f"no tile for {n} in {candidates}")


def _split3(x):
    h = x.astype(bf16)
    r = x - h.astype(f32)
    m = r.astype(bf16)
    l = (r - m.astype(f32)).astype(bf16)
    return h, m, l


def _dot(a, b, dims=None):
    if dims is None:
        return jnp.dot(a, b, preferred_element_type=f32)
    return lax.dot_general(a, b, (dims, ((), ())), preferred_element_type=f32)


def _mm(a, b):
    return _dot(a.astype(bf16), b.astype(bf16))


def _mm_exact_lhs(a_bf16, b):
    bh, bm, bl = _split3(b)
    return _dot(a_bf16, bh) + _dot(a_bf16, bm) + _dot(a_bf16, bl)


def _mm_exact_rhs(a, b_bf16):
    ah = a.astype(bf16)
    am = (a - ah.astype(f32)).astype(bf16)
    return _dot(ah, b_bf16) + _dot(am, b_bf16)


def _head_sum_matrix(scale):
    r = lax.broadcasted_iota(jnp.int32, (LANES, LANES), 0) // HEAD_DIM
    c = lax.broadcasted_iota(jnp.int32, (LANES, LANES), 1) // HEAD_DIM
    return jnp.where(r == c, scale, 0.0).astype(bf16)


def _head_sums(x, g):
    w = x.shape[-1]
    parts = [_mm_exact_rhs(x[:, i:i + LANES], g) for i in range(0, w, LANES)]
    return parts[0] if len(parts) == 1 else jnp.concatenate(parts, axis=-1)


def _resident(a, n_grid):
    zeros = (0,) * a.ndim
    return pl.BlockSpec(a.shape, lambda *_: zeros, pipeline_mode=pl.Buffered(1))


def _in_proj_kernel(mf_ref, x_ref, g_ref, w_ref, o_ref):
    first = pl.program_id(1) == 0
    half = ROW_TILE // 2
    us = []
    for r in (slice(0, half), slice(half, ROW_TILE)):
        x = jnp.where(first, mf_ref[r, :], x_ref[r, :])
        ms = jnp.mean(x * x, axis=-1, keepdims=True)
        us.append((x * lax.rsqrt(ms + NORM_EPS) * g_ref[...]).astype(bf16))
    for r, u in zip((slice(0, half), slice(half, ROW_TILE)), us):
        o_ref[r, :] = jnp.dot(u, w_ref[...], preferred_element_type=f32)


def _in_proj(meta_frame, x, g, w):
    b, n_tok, d = x.shape
    nx = n_tok // ROW_TILE
    return pl.pallas_call(
        _in_proj_kernel,
        grid=(b, nx + 1),
        in_specs=[_resident(meta_frame, 2),
                  pl.BlockSpec((None, ROW_TILE, d), lambda bi, i: (bi, jnp.maximum(i - 1, 0), 0)),
                  _resident(g, 2), _resident(w, 2)],
        out_specs=pl.BlockSpec((None, ROW_TILE, PROJ_W), lambda bi, i: (bi, i, 0)),
        out_shape=jax.ShapeDtypeStruct((b, FRONT + n_tok, PROJ_W), f32),
        compiler_params=pltpu.CompilerParams(
            dimension_semantics=("parallel", "arbitrary"), vmem_limit_bytes=VMEM_LIMIT),
        name="in_proj",
    )(meta_frame, x, g, w)


def _norm_rope(x, gain, cos, sin_signed, gsum, scale):
    ms = _mm_exact_rhs(x * x, gsum)
    xn = x * lax.rsqrt(ms + NORM_EPS) * gain
    lane = lax.broadcasted_iota(jnp.int32, x.shape, 1)
    first_half = (lane % AXIS_DIM) < AXIS_FREQS
    partner = jnp.where(first_half,
                        pltpu.roll(xn, LANES - AXIS_FREQS, axis=1),
                        pltpu.roll(xn, AXIS_FREQS, axis=1))
    out = xn * cos + partner * sin_signed
    return out * scale if scale != 1.0 else out


def _kv_kernel(k_ref, v_ref, cos_ref, sin_ref, kg_ref, ko_ref, vo_ref):
    gsum = _head_sum_matrix(1.0 / HEAD_DIM)
    kn = _norm_rope(k_ref[...], kg_ref[...], cos_ref[...], sin_ref[...], gsum, 1.0)
    vf = v_ref[...]
    t = kn.shape[0]
    lane = lax.broadcasted_iota(jnp.int32, kn.shape, 1)
    key_row = lax.broadcasted_iota(jnp.int32, kn.shape, 0) + pl.program_id(1) * t
    extra = lane == HEAD_DIM
    k_aug = jnp.where(extra & (key_row < META_ROW0 - KEY_ROW0), MASK_BIAS, 0.0)
    v_aug = jnp.where(extra, 1.0, 0.0)
    for h in range(ATTN_KV_HEADS):
        kh = kn if h == 0 else pltpu.roll(kn, HEAD_DIM, axis=1)
        vh = vf if h == 0 else pltpu.roll(vf, HEAD_DIM, axis=1)
        ko_ref[h] = jnp.where(lane < HEAD_DIM, kh, k_aug).astype(bf16)
        vo_ref[h] = jnp.where(lane < HEAD_DIM, vh, v_aug).astype(bf16)


def _kv_prep(proj3, cos, sin, kg):
    b, lp, _ = proj3.shape
    lk = lp - KEY_ROW0
    t = _pick_tile(lk, (KEY_ROW0, LANES))
    off = KEY_ROW0 // t
    row = lambda w, cb: pl.BlockSpec((None, t, w), lambda bi, ti: (bi, ti + off, cb))
    tab = pl.BlockSpec((t, LANES), lambda bi, ti: (ti, 0))
    vec = pl.BlockSpec((1, LANES), lambda bi, ti: (0, 0))
    kv_out = pl.BlockSpec((None, ATTN_KV_HEADS, t, LANES), lambda bi, ti: (bi, 0, ti, 0))
    kv_shape = jax.ShapeDtypeStruct((b, ATTN_KV_HEADS, lk, LANES), bf16)
    return pl.pallas_call(
        _kv_kernel,
        grid=(b, lk // t),
        in_specs=[row(ATTN_KV_WIDTH, COL_K // ATTN_KV_WIDTH),
                  row(ATTN_KV_WIDTH, COL_V // ATTN_KV_WIDTH),
                  tab, tab, vec],
        out_specs=[kv_out, kv_out],
        out_shape=[kv_shape, kv_shape],
        compiler_params=pltpu.CompilerParams(dimension_semantics=("parallel", "parallel")),
        name="kv_prep",
    )(proj3, proj3, cos, sin, kg)


def _attn_stages(q_ref, cos_ref, sin_ref, qg_ref, k_ref, v_ref, o_ref):
    tq = Q_TILE
    lane = lax.broadcasted_iota(jnp.int32, (tq, LANES), 1)
    one_hot = jnp.where(lane == HEAD_DIM, 1.0, 0.0)
    n_sub = q_ref.shape[0] // tq
    gsum = _head_sum_matrix(1.0 / HEAD_DIM)
    qn = [_norm_rope(q_ref[:, i:i + LANES], qg_ref[...], cos_ref[...], sin_ref[...], gsum,
                     HEAD_DIM ** -0.5 * LOG2E) for i in range(0, ATTN_GROUP * HEAD_DIM, LANES)]
    yield

    def scores(i):
        rows = []
        for g in range(ATTN_GROUP):
            blk = qn[g // 2][i * tq:(i + 1) * tq]
            if g % 2:
                blk = pltpu.roll(blk, HEAD_DIM, axis=1)
            rows.append(jnp.where(lane < HEAD_DIM, blk, one_hot))
        q = jnp.concatenate(rows, axis=0).astype(bf16)
        return lax.dot_general(q, k_ref[...], (((1,), (1,)), ((), ())), preferred_element_type=f32)

    def probs(s):
        return jnp.exp2(s - jnp.max(s, axis=-1, keepdims=True)).astype(bf16)

    def finish(i, p):
        ov = jnp.dot(p, v_ref[...], preferred_element_type=f32)
        on = ov / ov[:, HEAD_DIM:HEAD_DIM + 1]
        for g in range(0, ATTN_GROUP, 2):
            even = on[g * tq:(g + 1) * tq]
            odd = pltpu.roll(on[(g + 1) * tq:(g + 2) * tq], HEAD_DIM, axis=1)
            o_ref[i * tq:(i + 1) * tq, (g // 2) * LANES:(g // 2 + 1) * LANES] = jnp.where(
                lane < HEAD_DIM, even, odd).astype(o_ref.dtype)

    s, p = {}, {}
    for t in range(n_sub + 2):
        if t < n_sub:
            s[t] = scores(t)
        if 0 <= t - 1 < n_sub:
            p[t - 1] = probs(s.pop(t - 1))
        if 0 <= t - 2 < n_sub:
            finish(t - 2, p.pop(t - 2))
        yield


def _attn_specs(n_tok, lk):
    gw = ATTN_GROUP * HEAD_DIM
    tq = Q_TILE * Q_SUB
    nq = n_tok // tq
    q_row0, q_col0 = FRONT // tq, COL_Q // gw
    tile = lambda s: jnp.minimum(s, ATTN_KV_HEADS * nq - 1)
    head = lambda s: tile(s) // nq
    qi = lambda s: tile(s) % nq
    tab = pl.BlockSpec((tq, LANES), lambda bi, s: (qi(s), 0))
    kv = pl.BlockSpec((None, None, lk, LANES), lambda bi, s: (bi, head(s), 0, 0))
    ins = [pl.BlockSpec((None, tq, gw), lambda bi, s: (bi, qi(s) + q_row0, head(s) + q_col0)),
           tab, tab, pl.BlockSpec((1, LANES), lambda bi, s: (0, 0)), kv, kv]
    out = pl.BlockSpec((None, tq, gw), lambda bi, s: (bi, qi(s), head(s)))
    return ins, out


def _shifted_rows(p, prev_row, next_row):
    rows = lax.broadcasted_iota(jnp.int32, p.shape, 0)
    n = p.shape[0]
    prev = jnp.where(rows == 0, prev_row, pltpu.roll(p, 1, axis=0))
    nxt = jnp.where(rows == n - 1, next_row, pltpu.roll(p, n - 1, axis=0))
    return prev, nxt


def _stack_pair(x):
    lane = lax.broadcasted_iota(jnp.int32, x.shape, 1)
    first = lane < HEAD_DIM
    return jnp.concatenate([jnp.where(first, x, 0.0), jnp.where(first, 0.0, x)], axis=0)


STASH_NAMES = ("a", "b", "k", "v", "be", "ke")


class _LazyLoads:
    def __init__(self, load, n):
        self._load, self._n = load, n

    def __len__(self):
        return self._n

    def __getitem__(self, u):
        return self._load(u)

    def __iter__(self):
        return (self._load(u) for u in range(self._n))


def _chunk_terms(direction, p, prev_row, next_row, chunk_idx, prm, out):
    (mu_ref, w0_ref, w2_ref, a0_ref, a2_ref, g2_ref, kk_ref, ka_ref, rk_ref) = prm
    mu = mu_ref[...]
    prev, nxt = _shifted_rows(p, prev_row, next_row)
    z = p + mu[0:1] * (prev - p) + mu[1:2] * (nxt - p)

    rows = lax.broadcasted_iota(jnp.int32, (SCAN_ROWS, 1), 0) + chunk_idx * SCAN_ROWS
    valid = rows >= META_ROW0
    z = jnp.where(valid, z, 0.0)
    yield

    r = z[:, OFF_R:OFF_R + RWKV_WIDTH]
    k = z[:, OFF_K:OFF_K + RWKV_WIDTH]
    v = z[:, OFF_V:OFF_V + RWKV_WIDTH]
    lo, hi = direction * RWKV_WIDTH, (direction + 1) * RWKV_WIDTH

    x_w = w0_ref[...] + _mm(jnp.tanh(z[:, OFF_WD:OFF_AD]), w2_ref[...])
    logw = -jnp.exp(f32(-0.5)) * jax.nn.sigmoid(x_w[:, lo:hi])
    logw = jnp.where(valid, logw, 0.0)
    a_all = jax.nn.sigmoid(a0_ref[...] + _mm(z[:, OFF_AD:OFF_GD], a2_ref[...]))
    a_dir = a_all[:, lo:hi]
    yield

    gsum = _head_sum_matrix(1.0)
    kk = k * kk_ref[...]
    kk = kk / jnp.maximum(jnp.sqrt(_head_sums(kk * kk, gsum)), 1e-12)
    k_a = ka_ref[...]
    k_dir = k * (1.0 + (a_dir - 1.0) * k_a)
    kka = kk * a_dir
    yield

    if direction == 0:
        out["gate"] = _mm(jax.nn.sigmoid(z[:, OFF_GD:RWKV_IN_PAD]), g2_ref[...])
        a_mean = 0.5 * (a_all[:, :RWKV_WIDTH] + a_all[:, RWKV_WIDTH:])
        k_mean = k * (1.0 + (a_mean - 1.0) * k_a)
        out["bonus"] = _head_sums(r * k_mean * rk_ref[...], gsum) * v
        yield

    tr = lax.broadcasted_iota(jnp.int32, (SCAN_ROWS, SCAN_ROWS), 0)
    tc = lax.broadcasted_iota(jnp.int32, (SCAN_ROWS, SCAN_ROWS), 1)
    tri = ((tr // CHUNK) == (tc // CHUNK)) & ((tc <= tr) if direction == 0 else (tc >= tr))
    lc = _mm_exact_lhs(jnp.where(tri, 1.0, 0.0).astype(bf16), logw)
    last = CHUNK - 1 if direction == 0 else 0
    ltots = [lc[c * CHUNK + last:c * CHUNK + last + 1, :] for c in range(SCAN_SUB)]
    out["e_tot"] = [jnp.exp(t) for t in ltots]
    ltot = jnp.concatenate([jnp.broadcast_to(t, (CHUNK, RWKV_WIDTH)) for t in ltots], axis=0)
    yield

    sl = lambda j: slice(j * PAIR, (j + 1) * PAIR)
    stack = lambda x, dt: [[_stack_pair(x[c * CHUNK:(c + 1) * CHUNK, sl(j)]).astype(dt)
                            for j in range(N_PAIRS)] for c in range(SCAN_SUB)]
    e_neg = jnp.exp(-lc)
    out["b"] = stack(kka * e_neg, bf16)
    yield
    out["k"] = stack(k_dir * e_neg, bf16)
    yield
    e_end = jnp.exp(ltot - lc)
    out["be"] = stack(kka * e_end, bf16)
    yield
    out["ke"] = stack(k_dir * e_end, bf16)
    yield
    out["a"] = stack(-kk * jnp.exp(lc - logw), bf16)
    yield
    out["r"] = stack(r * jnp.exp(lc), f32)
    yield
    out["v"] = stack(v, bf16)


def _scan_units(units, ops, r_s, e_tot, h0, out):
    sr = lax.broadcasted_iota(jnp.int32, (PAIR, PAIR), 0)
    sc = lax.broadcasted_iota(jnp.int32, (PAIR, PAIR), 1)
    same = (sr // CHUNK) == (sc // CHUNK)
    tt, ss = sr % CHUNK, sc % CHUNK
    strict = (same & (ss < tt), same & (ss > tt))
    incl = (same & (ss <= tt), same & (ss >= tt))
    eye = sr == sc
    eye_f = jnp.where(eye, 1.0, 0.0)
    levels = []
    m = 2
    while m < CHUNK:
        levels.append(((sr // (2 * m)) == (sc // (2 * m))) & ((sr // m) != (sc // m)))
        m *= 2
    base = (sr // 2) == (sc // 2)

    nt = ((1,), (1,))
    tn = ((0,), (0,))
    cat0 = lambda u, w: jnp.concatenate([u, w], axis=0)
    cat1 = lambda u, w: jnp.concatenate([u, w], axis=1)
    sl = lambda j: slice(j * PAIR, (j + 1) * PAIR)
    a_s, b_s, k_s, v_s, be_s, ke_s = (ops[n] for n in STASH_NAMES)

    a_ab, a_ak, a_rb, a_rk = [], [], [], []
    for u, (d, _, _) in enumerate(units):
        scores = _dot(cat0(a_s[u], r_s[u].astype(bf16)), cat0(b_s[u], k_s[u]), nt)
        a_ab.append(jnp.where(strict[d], scores[:PAIR, :PAIR], 0.0))
        a_ak.append(jnp.where(strict[d], scores[:PAIR, PAIR:], 0.0).astype(bf16))
        a_rb.append(jnp.where(incl[d], scores[PAIR:, :PAIR], 0.0).astype(bf16))
        a_rk.append(jnp.where(incl[d], scores[PAIR:, PAIR:], 0.0).astype(bf16))
    yield

    t_inv = [eye_f + jnp.where(base, a, 0.0) for a in a_ab]

    akv = [_dot(cat0(x, y), v) for x, y, v in zip(a_ak, a_rk, v_s)]
    av = [x[:PAIR] for x in akv]
    y0b = [x[PAIR:] for x in akv]
    kv = [_dot(k, v, tn) for k, v in zip(ke_s, v_s)]
    yield
    for off in levels:
        tb = [t.astype(bf16) for t in t_inv]
        at = [_dot(jnp.where(off, a, 0.0).astype(bf16), t) for a, t in zip(a_ab, tb)]
        yield
        t_inv = [t + _dot(t16, x.astype(bf16)) for t, t16, x in zip(t_inv, tb, at)]
        yield

    w2 = [_dot(t.astype(bf16), cat1(a, x.astype(bf16))).astype(bf16)
          for t, a, x in zip(t_inv, a_s, av)]
    yield
    qy = [_dot(a, w) for a, w in zip(a_rb, w2)]
    mn = [_dot(b, w, tn) for b, w in zip(be_s, w2)]
    yield
    qh = [cat0(r + x[:, :PAIR], y[:, :PAIR]).astype(bf16) for r, x, y in zip(r_s, qy, mn)]
    yield
    out["y"], out["h"] = {}, dict(h0)
    for pos in range(SCAN_SUB):
        for u, (d, c, j) in enumerate(units):
            if c != (pos if d == 0 else SCAN_SUB - 1 - pos):
                continue
            h = out["h"][d, j]
            qm = _dot(qh[u], h.astype(bf16))
            y_st = qm[:PAIR] + qy[u][:, PAIR:] + y0b[u]
            out["y"][d, c, j] = y_st[:CHUNK] + y_st[CHUNK:]
            decay_col = jnp.sum(jnp.where(eye, e_tot[d][c][:, sl(j)], 0.0), axis=1, keepdims=True)
            out["h"][d, j] = decay_col * h + qm[PAIR:] + mn[u][:, PAIR:] + kv[u]
        yield


def _mixer_kernel(pf_ref, pf_prev_ref, pf_next_ref, pb_ref, pb_prev_ref, pb_next_ref,
                  mu_ref, w0_ref, w2_ref, a0_ref, a2_ref, g2_ref, kk_ref, ka_ref, rk_ref,
                  q_ref, cos_ref, sin_ref, qg_ref, k_ref, v_ref,
                  yf_ref, yb_ref, gate_ref, bonus_ref, att_ref,
                  h_ref, ops_ref, r_ref, e_ref, *, n_blocks):
    s = pl.program_id(1)

    @pl.when(s == 0)
    def _():
        ops_ref[...] = jnp.zeros_like(ops_ref)
        r_ref[...] = jnp.zeros_like(r_ref)
        e_ref[...] = jnp.zeros_like(e_ref)

    @pl.when(s <= 1)
    def _():
        h_ref[...] = jnp.zeros_like(h_ref)

    units = [(d, c, j) for d in range(2) for c in range(SCAN_SUB) for j in range(N_PAIRS)]
    sl = lambda j: slice(j * PAIR, (j + 1) * PAIR)
    rows = lambda c: slice(c * CHUNK, (c + 1) * CHUNK)

    ops = {n: _LazyLoads(lambda u, i=i: ops_ref[(units[u][0], i) + units[u][1:]], len(units))
           for i, n in enumerate(STASH_NAMES)}
    e_tot = [[e_ref[d, c] for c in range(SCAN_SUB)] for d in range(2)]
    h0 = {(d, j): h_ref[d, j] for d in range(2) for j in range(N_PAIRS)}
    scan = {}
    scan_gen = _scan_units(units, ops, _LazyLoads(lambda u: r_ref[units[u]], len(units)),
                           e_tot, h0, scan)

    prm = (mu_ref, w0_ref, w2_ref, a0_ref, a2_ref, g2_ref, kk_ref, ka_ref, rk_ref)
    terms = [{}, {}]
    stages = [scan_gen, _attn_stages(q_ref, cos_ref, sin_ref, qg_ref, k_ref, v_ref, att_ref)]
    for direction, (p_ref, prev_ref, next_ref) in enumerate(
            ((pf_ref, pf_prev_ref, pf_next_ref), (pb_ref, pb_prev_ref, pb_next_ref))):
        bi = jnp.minimum(s, n_blocks - 1) if direction == 0 else jnp.maximum(n_blocks - 1 - s, 0)
        prev_row = jnp.where(bi == 0, 0.0, prev_ref[HALO - 1:HALO, :])
        next_row = jnp.where(bi == n_blocks - 1, 0.0, next_ref[0:1, :])
        stages.append(_chunk_terms(direction, p_ref[...], prev_row, next_row, bi + BLOCK0, prm,
                                   terms[direction]))
    while stages:
        for g in list(stages):
            if next(g, stages) is stages:
                stages.remove(g)

    for (d, j), h in scan["h"].items():
        h_ref[d, j] = h
    for d, c, j in units:
        for i, n in enumerate(STASH_NAMES):
            ops_ref[d, i, c, j] = terms[d][n][c][j]
        r_ref[d, c, j] = terms[d]["r"][c][j]
    for d in range(2):
        for c in range(SCAN_SUB):
            e_ref[d, c] = terms[d]["e_tot"][c]

    @pl.when((s >= 1) & (s <= n_blocks - 1))
    def _():
        gate_ref[...] = terms[0]["gate"].astype(gate_ref.dtype)
        bonus_ref[...] = terms[0]["bonus"].astype(bonus_ref.dtype)

    @pl.when(s >= 2)
    def _():
        for d, c, j in units:
            if d == 0:
                yf_ref[rows(c), sl(j)] = scan["y"][d, c, j].astype(yf_ref.dtype)

    @pl.when((s >= 1) & (s <= n_blocks - 1))
    def _():
        for d, c, j in units:
            if d == 1:
                yb_ref[rows(c), sl(j)] = scan["y"][d, c, j].astype(yb_ref.dtype)


def _mixers(proj3, n_tok, mu, w0, w2, a0, a2, g2, k_k, k_a, r_k, cos, sin, qg, k4, v4):
    b, lp, _ = proj3.shape
    nb = lp // SCAN_ROWS - BLOCK0
    hb = SCAN_ROWS // HALO
    cb = COL_RWKV // RWKV_IN_PAD
    h_last = (lp // SCAN_ROWS) * hb - 1
    block_of = (lambda s: jnp.minimum(s, nb - 1) + BLOCK0,
                lambda s: jnp.maximum(nb - 1 - s, 0) + BLOCK0)
    main = lambda d: pl.BlockSpec((None, SCAN_ROWS, RWKV_IN_PAD),
                                  lambda bi, s: (bi, block_of[d](s), cb))
    prev = lambda d: pl.BlockSpec((None, HALO, RWKV_IN_PAD),
                                  lambda bi, s: (bi, block_of[d](s) * hb - 1, cb))
    nxt = lambda d: pl.BlockSpec(
        (None, HALO, RWKV_IN_PAD),
        lambda bi, s: (bi, jnp.minimum((block_of[d](s) + 1) * hb, h_last), cb))
    blk = lambda f: pl.BlockSpec((None, SCAN_ROWS, RWKV_WIDTH), lambda bi, s: (bi, f(s), 0))
    y_fwd = blk(lambda s: jnp.maximum(s - 2, 0))
    y_bwd = blk(lambda s: jnp.clip(nb - s - 1, 0, nb - 2))
    prep = blk(lambda s: jnp.clip(s - 1, 0, nb - 2))
    shp = jax.ShapeDtypeStruct((b, n_tok, RWKV_WIDTH), bf16)
    params = (mu, w0, w2, a0, a2, g2, k_k, k_a, r_k)
    n_steps = nb + 1
    assert n_steps >= ATTN_KV_HEADS * n_tok // (Q_TILE * Q_SUB)
    attn_in, attn_out = _attn_specs(n_tok, k4.shape[2])
    return pl.pallas_call(
        functools.partial(_mixer_kernel, n_blocks=nb),
        grid=(b, n_steps),
        in_specs=[main(0), prev(0), nxt(0), main(1), prev(1), nxt(1)]
                 + [_resident(a, 2) for a in params] + attn_in,
        out_specs=[y_fwd, y_bwd, prep, prep, attn_out],
        out_shape=[shp, shp, shp, shp, jax.ShapeDtypeStruct((b, n_tok, ATTN_Q_WIDTH), bf16)],
        scratch_shapes=[pltpu.VMEM((2, N_PAIRS, PAIR, PAIR), f32),
                        pltpu.VMEM((2, len(STASH_NAMES), SCAN_SUB, N_PAIRS, PAIR, PAIR), bf16),
                        pltpu.VMEM((2, SCAN_SUB, N_PAIRS, PAIR, PAIR), f32),
                        pltpu.VMEM((2, SCAN_SUB, 1, RWKV_WIDTH), f32)],
        compiler_params=pltpu.CompilerParams(
            dimension_semantics=("parallel", "arbitrary"), vmem_limit_bytes=VMEM_LIMIT),
        name="mixers",
    )(proj3, proj3, proj3, proj3, proj3, proj3, *params, proj3, cos, sin, qg, k4, v4)


def _post_kernel(yf_ref, yb_ref, gate_ref, bonus_ref, att_ref, ga_ref, gb_ref, h_ref,
                 lng_ref, lnb_ref, wa_ref, wb_ref, wo_ref, o_ref):
    gmean = _head_sum_matrix(1.0 / HEAD_DIM)
    half = ROW_TILE // 2
    parts = (slice(0, half), slice(half, ROW_TILE))
    out_a = []
    for r in parts:
        y = yf_ref[r, :].astype(f32) + yb_ref[r, :].astype(f32)
        mu = _head_sums(y, gmean)
        d = y - mu
        var = _head_sums(d * d, gmean)
        yn = d * lax.rsqrt(var + LNX_EPS) * lng_ref[...] + lnb_ref[...]
        out_a.append(((yn + bonus_ref[r, :].astype(f32)) * gate_ref[r, :].astype(f32)).astype(bf16))
    ya = [jnp.dot(a, wa_ref[...], preferred_element_type=f32) for a in out_a]
    yb = [jnp.dot(att_ref[r, :], wb_ref[...], preferred_element_type=f32) for r in parts]
    merged = [(jax.nn.sigmoid(ga_ref[r, :]) * a + jax.nn.sigmoid(gb_ref[r, :]) * c).astype(bf16)
              for r, a, c in zip(parts, ya, yb)]
    for r, m in zip(parts, merged):
        o_ref[r, :] = h_ref[r, :] + jnp.dot(m, wo_ref[...], preferred_element_type=f32)


def _post(yf, yb, gate, bonus, att, proj3, x, lng, lnb, wa, wb, wo):
    b, n_tok, d = x.shape
    tm = ROW_TILE
    off = FRONT // tm
    row = lambda w: pl.BlockSpec((None, tm, w), lambda bi, i: (bi, i, 0))
    gate_cols = lambda cb: pl.BlockSpec((None, tm, d), lambda bi, i: (bi, i + off, cb))
    return pl.pallas_call(
        _post_kernel,
        grid=(b, n_tok // tm),
        in_specs=[row(RWKV_WIDTH), row(RWKV_WIDTH), row(RWKV_WIDTH), row(RWKV_WIDTH),
                  row(ATTN_Q_WIDTH), gate_cols(COL_GATE_A // d), gate_cols(COL_GATE_B // d), row(d),
                  _resident(lng, 2), _resident(lnb, 2), _resident(wa, 2), _resident(wb, 2),
                  _resident(wo, 2)],
        out_specs=pl.BlockSpec((None, tm, d), lambda bi, i: (bi, i, 0)),
        out_shape=jax.ShapeDtypeStruct((b, n_tok, d), f32),
        compiler_params=pltpu.CompilerParams(
            dimension_semantics=("parallel", "parallel"), vmem_limit_bytes=VMEM_LIMIT),
        name="post",
    )(yf, yb, gate, bonus, att, proj3, proj3, x, lng, lnb, wa, wb, wo)


def _ffn_kernel(h_ref, g_ref, w1_ref, w2_ref, gf_ref, o_ref, *, ff_chunk):
    h = h_ref[...]
    ms = jnp.mean(h * h, axis=-1, keepdims=True)
    xn = (h * lax.rsqrt(ms + NORM_EPS) * g_ref[...]).astype(bf16)
    acc = h
    for c in range(0, w1_ref.shape[1], ff_chunk):
        f = jnp.dot(xn, w1_ref[:, c:c + ff_chunk], preferred_element_type=f32)
        f = jnp.square(jnp.maximum(f, 0.0)).astype(bf16)
        acc = acc + jnp.dot(f, w2_ref[c:c + ff_chunk, :], preferred_element_type=f32)
    ms2 = jnp.mean(acc * acc, axis=-1, keepdims=True)
    o_ref[...] = acc * lax.rsqrt(ms2 + NORM_EPS) * gf_ref[...]


def _ffn(h2d, g, w1, w2, gf):
    n, d = h2d.shape
    tm = _pick_tile(n, (1024, ROW_TILE))
    row = pl.BlockSpec((tm, d), lambda i: (i, 0))
    return pl.pallas_call(
        functools.partial(_ffn_kernel, ff_chunk=512),
        grid=(n // tm,),
        in_specs=[row, _resident(g, 1), _resident(w1, 1), _resident(w2, 1), _resident(gf, 1)],
        out_specs=row,
        out_shape=jax.ShapeDtypeStruct((n, d), f32),
        compiler_params=pltpu.CompilerParams(
            dimension_semantics=("parallel",), vmem_limit_bytes=VMEM_LIMIT),
        name="ffn",
    )(h2d, g, w1, w2, gf)


def _rope_tables(n_tok, lk):
    rows = n_tok // GRID_W
    inv_freq = ROPE_THETA ** (-jnp.arange(AXIS_FREQS, dtype=f32) * 2.0 / AXIS_DIM)
    row_ang = jnp.arange(rows, dtype=f32)[:, None] * inv_freq
    col_ang = jnp.arange(GRID_W, dtype=f32)[:, None] * inv_freq
    grid = jnp.stack([jnp.broadcast_to(row_ang[:, None, :], (rows, GRID_W, AXIS_FREQS)),
                      jnp.broadcast_to(col_ang[None, :, :], (rows, GRID_W, AXIS_FREQS))], axis=2)
    grid = grid.reshape(rows * GRID_W, 2, AXIS_FREQS)
    ang = jnp.concatenate([jnp.zeros((lk - n_tok, 2, AXIS_FREQS), f32), grid], axis=0)
    cos, sin = jnp.cos(ang), jnp.sin(ang)
    cos64 = jnp.stack([cos, cos], axis=2).reshape(lk, HEAD_DIM)
    sin64 = jnp.stack([-sin, sin], axis=2).reshape(lk, HEAD_DIM)
    return jnp.tile(cos64, (1, 2)), jnp.tile(sin64, (1, 2))


def kernel(x, meta_tokens, mix_norm_g, w_in, rwkv_shift, decay_w0, decay_w2, icl_a0, icl_a2, gate_w2, k_k, k_a, r_k, lnx_g, lnx_b, q_norm_g, k_norm_g, w_branch_rwkv, w_branch_attn, w_out, ffn_norm_g, w_ff1, w_ff2, final_norm_g):
    b, n_tok, d = x.shape
    assert n_tok % ROW_TILE == 0 and d == COL_GATE_B
    lp = FRONT + n_tok
    lk = lp - KEY_ROW0

    w = w_in[0]
    s0 = RWKV_IN
    s1 = s0 + ATTN_Q_WIDTH
    s2 = s1 + ATTN_KV_WIDTH
    s3 = s2 + ATTN_KV_WIDTH
    s4 = s3 + d
    w_perm = jnp.concatenate(
        [w[:, s3:s4], w[:, s4:], w[:, :s0], jnp.zeros((d, RWKV_IN_PAD - RWKV_IN), w.dtype),
         w[:, s0:s1], w[:, s1:s2], w[:, s2:s3]], axis=1).astype(bf16)
    mu = jnp.pad(rwkv_shift[0], ((0, 0), (0, RWKV_IN_PAD - RWKV_IN)))
    zl = jnp.zeros((DECAY_LORA, RWKV_WIDTH), f32)
    w2cat = jnp.concatenate([jnp.concatenate([decay_w2[0, 0], zl], axis=1),
                             jnp.concatenate([zl, decay_w2[0, 1]], axis=1)], axis=0)
    a2cat = jnp.concatenate([jnp.concatenate([icl_a2[0, 0], zl], axis=1),
                             jnp.concatenate([zl, icl_a2[0, 1]], axis=1)], axis=0)
    w0cat = decay_w0[0].reshape(1, 2 * RWKV_WIDTH)
    a0cat = icl_a0[0].reshape(1, 2 * RWKV_WIDTH)
    g2pad = jnp.pad(gate_w2[0], ((0, GD_PAD - GATE_LORA), (0, 0)))
    row = lambda a: a.reshape(1, -1)

    meta_frame = jnp.concatenate([jnp.zeros((META_ROW0, d), x.dtype), meta_tokens.astype(x.dtype)])
    proj3 = _in_proj(meta_frame, x, row(mix_norm_g[0]), w_perm)

    cos, sin = _rope_tables(n_tok, lk)
    qg = jnp.tile(row(q_norm_g[0]), (1, 2))
    kg = jnp.tile(row(k_norm_g[0]), (1, 2))
    tok0 = FRONT - KEY_ROW0
    k4, v4 = _kv_prep(proj3, cos, sin, kg)
    yf, yb, gate, bonus, att = _mixers(proj3, n_tok, mu, w0cat, w2cat, a0cat, a2cat, g2pad,
                                       row(k_k[0]), row(k_a[0]), row(r_k[0]),
                                       cos[tok0:], sin[tok0:], qg, k4, v4)

    h1 = _post(yf, yb, gate, bonus, att, proj3, x, row(lnx_g[0]), row(lnx_b[0]),
               w_branch_rwkv[0].astype(bf16), w_branch_attn[0].astype(bf16),
               w_out[0].astype(bf16))
    out = _ffn(h1.reshape(b * n_tok, d), row(ffn_norm_g[0]), w_ff1[0].astype(bf16),
               w_ff2[0].astype(bf16), row(final_norm_g))
    return out.reshape(b, n_tok, d)
```
